```python
import jax, jax.numpy as jnp
from jax import lax
import numpy as np

D_MODEL = 1024
BATCH = 32
SEQ = 2048
DEPTH = 1
DEC_BATCH = 32
DEC_SEQ = 32
PAST_LEN = 4096

CHUNK = 64
D_MIX = D_MODEL
D_LRU = D_MIX // 2
D_CONV = D_MIX - D_LRU
LRU_HEADS = 8
LRU_HEAD_DIM = D_LRU // LRU_HEADS
LRU_CONV_W = 4
LRU_C = 8.0
CM_KERNEL = 31
D_FF = ((8 * D_MODEL // 3 + 127) // 128) * 128
N_MOD = 9
EPS = 1e-6
ADA_SCALE = 0.02

kernel_name = "hybrid_rglru_conformer_stream_step"


def rms_norm(x, g):
    xf = x.astype(jnp.float32)
    y = xf * lax.rsqrt(jnp.mean(xf * xf, axis=-1, keepdims=True) + EPS)
    return (y * g.astype(jnp.float32)).astype(x.dtype)


def layer_norm(x, g, b):
    xf = x.astype(jnp.float32)
    mu = jnp.mean(xf, axis=-1, keepdims=True)
    xc = xf - mu
    y = xc * lax.rsqrt(jnp.mean(xc * xc, axis=-1, keepdims=True) + EPS)
    return (y * g.astype(jnp.float32) + b.astype(jnp.float32)).astype(x.dtype)


def modulate(x, g, shift, scale):
    return rms_norm(x, g) * (1 + scale[:, None, :]) + shift[:, None, :]


def swiglu(h, wg, wu, wd):
    return (jax.nn.silu(h @ wg) * (h @ wu)) @ wd


def causal_dwconv(x_full, w, b):
    C = x_full.shape[-1]
    out = lax.conv_general_dilated(
        x_full, w[:, None, :].astype(x_full.dtype), window_strides=(1,), padding="VALID",
        dimension_numbers=("NWC", "WIO", "NWC"), feature_group_count=C)
    return out + b


def block_diag(y, w):
    B, T, _ = y.shape
    yh = y.reshape(B, T, LRU_HEADS, LRU_HEAD_DIM)
    return jnp.einsum("bthi,hij->bthj", yh, w).reshape(B, T, D_LRU)


def rg_lru(y, z_gate, h0, wa, ba, wx, bx, lam):
    yf = y.astype(jnp.float32)
    r = jax.nn.sigmoid(block_diag(y, wa) + ba).astype(jnp.float32)
    i = jax.nn.sigmoid(block_diag(y, wx) + bx).astype(jnp.float32)
    log_a = -LRU_C * r * jax.nn.softplus(-lam.astype(jnp.float32))
    a = jnp.exp(log_a)
    b = jnp.sqrt(-jnp.expm1(2.0 * log_a)) * (i * yf)

    def combine(left, right):
        a1, b1 = left
        a2, b2 = right
        return a1 * a2, a2 * b1 + b2

    a_cum, b_cum = lax.associative_scan(combine, (a, b), axis=1)
    h = a_cum * h0.astype(jnp.float32)[:, None, :] + b_cum
    out = h.astype(y.dtype) * jax.nn.gelu(z_gate)
    return out, h[:, -1].astype(h0.dtype)


def stream_layer(x, c, lru_h0, lru_buf, cm_buf, w_ada, b_ada,
                 ffn1_g, ffn1_wg, ffn1_wu, ffn1_wd, mix_g, w_in,
                 lru_conv_w, lru_conv_b, lru_wa, lru_ba, lru_wx, lru_bx, lru_lambda,
                 cm_dw_w, cm_dw_b, cm_ln_g, cm_ln_b, out_g_lru, out_g_conv, w_out,
                 ffn2_g, ffn2_wg, ffn2_wu, ffn2_wd):
    B = c.shape[0]
    mod = (jax.nn.silu(c) @ w_ada + b_ada).reshape(B, N_MOD, D_MODEL)
    sh1, sc1, gt1, sh2, sc2, gt2, sh3, sc3, gt3 = [mod[:, k] for k in range(N_MOD)]

    h = modulate(x, ffn1_g, sh1, sc1)
    x = x + 0.5 * gt1[:, None, :] * swiglu(h, ffn1_wg, ffn1_wu, ffn1_wd)

    h = modulate(x, mix_g, sh2, sc2)
    z = h @ w_in
    z_lx, z_lg, z_cv, z_cg = jnp.split(z, [D_LRU, 2 * D_LRU, 2 * D_LRU + D_CONV], axis=-1)

    lru_full = jnp.concatenate([lru_buf.astype(z_lx.dtype), z_lx], axis=1)
    y_l = causal_dwconv(lru_full, lru_conv_w, lru_conv_b)
    lru_out, h_last = rg_lru(y_l, z_lg, lru_h0, lru_wa, lru_ba, lru_wx, lru_bx, lru_lambda)

    u = z_cv * jax.nn.sigmoid(z_cg)
    cm_full = jnp.concatenate([cm_buf.astype(u.dtype), u], axis=1)
    v = causal_dwconv(cm_full, cm_dw_w, cm_dw_b)
    v = jax.nn.silu(layer_norm(v, cm_ln_g, cm_ln_b))

    merged = jnp.concatenate([rms_norm(lru_out, out_g_lru), rms_norm(v, out_g_conv)], axis=-1) @ w_out
    x = x + gt2[:, None, :] * merged

    h = modulate(x, ffn2_g, sh3, sc3)
    x = x + 0.5 * gt3[:, None, :] * swiglu(h, ffn2_wg, ffn2_wu, ffn2_wd)
    return x, h_last, lru_full[:, -(LRU_CONV_W - 1):], cm_full[:, -(CM_KERNEL - 1):]


def setup_inputs(seed: int = 0) -> dict:
    key = jax.random.key(seed)
    ks = jax.random.split(key, 40)

    def nrm(k, shape, scale):
        return jax.random.normal(k, shape, jnp.float32) * scale

    def gain(k, shape):
        return 1.0 + 0.1 * jax.random.normal(k, shape, jnp.float32)

    u = jax.random.uniform(ks[21], (DEPTH, D_LRU), jnp.float32, minval=0.9, maxval=0.999)
    a_base = u ** (1.0 / LRU_C)
    lru_lambda = jnp.log(a_base) - jnp.log1p(-a_base)

    return {
        "x_prompt": nrm(ks[0], (BATCH, SEQ, D_MODEL), 1.0),
        "x_sample": nrm(ks[1], (DEC_BATCH, DEC_SEQ, D_MODEL), 1.0),
        "state_lru_h": nrm(ks[2], (DEPTH, DEC_BATCH, D_LRU), 0.5),
        "state_lru_conv": nrm(ks[3], (DEPTH, DEC_BATCH, LRU_CONV_W - 1, D_LRU), 1.0),
        "state_cm_conv": nrm(ks[4], (DEPTH, DEC_BATCH, CM_KERNEL - 1, D_CONV), 1.0),
        "c_prompt": nrm(ks[5], (BATCH, D_MODEL), 1.0),
        "c_sample": nrm(ks[6], (DEC_BATCH, D_MODEL), 1.0),
        "w_ada": nrm(ks[7], (DEPTH, D_MODEL, N_MOD * D_MODEL), ADA_SCALE),
        "b_ada": nrm(ks[8], (DEPTH, N_MOD * D_MODEL), 0.01),
        "ffn1_g": gain(ks[9], (DEPTH, D_MODEL)),
        "ffn1_wg": nrm(ks[10], (DEPTH, D_MODEL, D_FF), D_MODEL ** -0.5),
        "ffn1_wu": nrm(ks[11], (DEPTH, D_MODEL, D_FF), D_MODEL ** -0.5),
        "ffn1_wd": nrm(ks[12], (DEPTH, D_FF, D_MODEL), D_FF ** -0.5),
        "mix_g": gain(ks[13], (DEPTH, D_MODEL)),
        "w_in": nrm(ks[14], (DEPTH, D_MODEL, 2 * D_LRU + 2 * D_CONV), D_MODEL ** -0.5),
        "lru_conv_w": nrm(ks[15], (DEPTH, LRU_CONV_W, D_LRU), LRU_CONV_W ** -0.5),
        "lru_conv_b": nrm(ks[16], (DEPTH, D_LRU), 0.01),
        "lru_wa": nrm(ks[17], (DEPTH, LRU_HEADS, LRU_HEAD_DIM, LRU_HEAD_DIM), LRU_HEAD_DIM ** -0.5),
        "lru_ba": nrm(ks[18], (DEPTH, D_LRU), 0.01),
        "lru_wx": nrm(ks[19], (DEPTH, LRU_HEADS, LRU_HEAD_DIM, LRU_HEAD_DIM), LRU_HEAD_DIM ** -0.5),
        "lru_bx": nrm(ks[20], (DEPTH, D_LRU), 0.01),
        "lru_lambda": lru_lambda,
        "cm_dw_w": nrm(ks[22], (DEPTH, CM_KERNEL, D_CONV), CM_KERNEL ** -0.5),
        "cm_dw_b": nrm(ks[23], (DEPTH, D_CONV), 0.01),
        "cm_ln_g": gain(ks[24], (DEPTH, D_CONV)),
        "cm_ln_b": nrm(ks[25], (DEPTH, D_CONV), 0.01),
        "out_g_lru": gain(ks[26], (DEPTH, D_LRU)),
        "out_g_conv": gain(ks[27], (DEPTH, D_CONV)),
        "w_out": nrm(ks[28], (DEPTH, D_MIX, D_MODEL), D_MIX ** -0.5),
        "ffn2_g": gain(ks[29], (DEPTH, D_MODEL)),
        "ffn2_wg": nrm(ks[30], (DEPTH, D_MODEL, D_FF), D_MODEL ** -0.5),
        "ffn2_wu": nrm(ks[31], (DEPTH, D_MODEL, D_FF), D_MODEL ** -0.5),
        "ffn2_wd": nrm(ks[32], (DEPTH, D_FF, D_MODEL), D_FF ** -0.5),
        "final_g": gain(ks[33], (D_MODEL,)),
    }


def reference(x_prompt, x_sample, state_lru_h, state_lru_conv, state_cm_conv, c_prompt, c_sample,
              w_ada, b_ada, ffn1_g, ffn1_wg, ffn1_wu, ffn1_wd, mix_g, w_in,
              lru_conv_w, lru_conv_b, lru_wa, lru_ba, lru_wx, lru_bx, lru_lambda,
              cm_dw_w, cm_dw_b, cm_ln_g, cm_ln_b, out_g_lru, out_g_conv, w_out,
              ffn2_g, ffn2_wg, ffn2_wu, ffn2_wd, final_g):
    xp, xs = x_prompt, x_sample
    Bp = x_prompt.shape[0]
    hp_list, lcp_list, ccp_list = [], [], []
    hs_list, lcs_list, ccs_list = [], [], []
    for l in range(DEPTH):
        w = (w_ada[l], b_ada[l], ffn1_g[l], ffn1_wg[l], ffn1_wu[l], ffn1_wd[l], mix_g[l], w_in[l],
             lru_conv_w[l], lru_conv_b[l], lru_wa[l], lru_ba[l], lru_wx[l], lru_bx[l], lru_lambda[l],
             cm_dw_w[l], cm_dw_b[l], cm_ln_g[l], cm_ln_b[l], out_g_lru[l], out_g_conv[l], w_out[l],
             ffn2_g[l], ffn2_wg[l], ffn2_wu[l], ffn2_wd[l])
        h0_p = jnp.zeros((Bp, D_LRU), x_prompt.dtype)
        lbuf_p = jnp.zeros((Bp, LRU_CONV_W - 1, D_LRU), x_prompt.dtype)
        cbuf_p = jnp.zeros((Bp, CM_KERNEL - 1, D_CONV), x_prompt.dtype)
        xp, hp, lcp, ccp = stream_layer(xp, c_prompt, h0_p, lbuf_p, cbuf_p, *w)
        xs, hs, lcs, ccs = stream_layer(xs, c_sample, state_lru_h[l], state_lru_conv[l], state_cm_conv[l], *w)
        hp_list.append(hp); lcp_list.append(lcp); ccp_list.append(ccp)
        hs_list.append(hs); lcs_list.append(lcs); ccs_list.append(ccs)
    y_prompt = rms_norm(xp, final_g)
    y_sample = rms_norm(xs, final_g)
    new_lru_h_p = jnp.stack(hp_list, axis=0)
    new_lru_conv_p = jnp.stack(lcp_list, axis=0)
    new_cm_conv_p = jnp.stack(ccp_list, axis=0)
    new_lru_h_s = jnp.stack(hs_list, axis=0)
    new_lru_conv_s = jnp.stack(lcs_list, axis=0)
    new_cm_conv_s = jnp.stack(ccs_list, axis=0)
    return (y_prompt, y_sample, new_lru_h_p, new_lru_conv_p, new_cm_conv_p, new_lru_h_s, new_lru_conv_s, new_cm_conv_s)
```

```python
import functools

import jax
import jax.numpy as jnp
from jax.experimental import pallas as pl
from jax.experimental.pallas import tpu as pltpu

EPS = 1e-6
LRU_C = 8.0
N_MOD = 9

V7X_SUBLANES = 8
V7X_MXU_DIM = 256
V7X_VMEM_LIMIT_BYTES = 56 * 1024 * 1024

F32 = jnp.float32
BF16 = jnp.bfloat16


def _round_up(n, m):
    return (n + m - 1) // m * m


def _dot(a, b):
    return jnp.dot(a, b, preferred_element_type=F32)


def _rms(xf, g):
    return xf * jax.lax.rsqrt(jnp.mean(xf * xf, axis=-1, keepdims=True) + EPS) * g


def _const_spec(shape):
    zeros = (0,) * len(shape)
    return pl.BlockSpec(shape, lambda *_: zeros, pipeline_mode=pl.Buffered(1))


def _ada_body(c_ref, w_ref, b_ref, o_ref):
    c = c_ref[...]
    s = (c * jax.nn.sigmoid(c)).astype(BF16)
    o_ref[...] = _dot(s, w_ref[...].astype(BF16)) + b_ref[...]


def _ada(c, w_ada, b_ada):
    bsz, d = c.shape
    n = w_ada.shape[1]
    tn = d
    return pl.pallas_call(
        _ada_body,
        grid=(n // tn,),
        in_specs=[
            pl.BlockSpec((bsz, d), lambda j: (0, 0)),
            pl.BlockSpec((d, tn), lambda j: (0, j)),
            pl.BlockSpec((1, tn), lambda j: (0, j)),
        ],
        out_specs=pl.BlockSpec((bsz, tn), lambda j: (0, j)),
        out_shape=jax.ShapeDtypeStruct((bsz, n), F32),
        compiler_params=pltpu.CompilerParams(
            dimension_semantics=("arbitrary",), vmem_limit_bytes=V7X_VMEM_LIMIT_BYTES),
        name="ada_mod",
    )(c, w_ada, b_ada.reshape(1, n))


def _ffn_chunks(d_ff, width):
    out, lo = [], 0
    while lo < d_ff:
        sz = min(width, d_ff - lo)
        out.append((lo, sz))
        lo += sz
    return tuple(out)


def _ffn_body(*refs, mod_base, chunks, final):
    if final:
        x_ref, mod_ref, g_ref, wg_ref, wu_ref, wd_ref, fg_ref, o_ref = refs
    else:
        x_ref, mod_ref, g_ref, wg_ref, wu_ref, wd_ref, o_ref = refs
    nb, tt, d = x_ref.shape
    x = x_ref[...]
    sh = mod_ref[:, mod_base:mod_base + 1, :]
    sc = mod_ref[:, mod_base + 1:mod_base + 2, :]
    gt = mod_ref[:, mod_base + 2:mod_base + 3, :]
    h = _rms(x, g_ref[...]) * (1.0 + sc) + sh
    hb = h.reshape(nb * tt, d).astype(BF16)
    acc = None
    for lo, sz in chunks:
        g = _dot(hb, wg_ref[:, lo:lo + sz])
        u = _dot(hb, wu_ref[:, lo:lo + sz])
        a = (g * jax.nn.sigmoid(g) * u).astype(BF16)
        p = _dot(a, wd_ref[lo:lo + sz, :])
        acc = p if acc is None else acc + p
    y = x + (0.5 * gt) * acc.reshape(nb, tt, d)
    if final:
        y = _rms(y, fg_ref[...])
    o_ref[...] = y


def _ffn(x, mod, g, wg, wu, wd, final_g, *, mod_base, nb, tt):
    bsz, t, d = x.shape
    d_ff = wg.shape[1]
    final = final_g is not None
    body = functools.partial(
        _ffn_body, mod_base=mod_base, chunks=_ffn_chunks(d_ff, 3 * V7X_MXU_DIM), final=final)
    xspec = pl.BlockSpec((nb, tt, d), lambda b, i: (b, i, 0))
    in_specs = [
        xspec,
        pl.BlockSpec((nb, N_MOD, d), lambda b, i: (b, 0, 0)),
        _const_spec((1, d)),
        _const_spec((d, d_ff)),
        _const_spec((d, d_ff)),
        _const_spec((d_ff, d)),
    ]
    args = [x, mod, g.reshape(1, d), wg, wu, wd]
    if final:
        in_specs.append(_const_spec((1, d)))
        args.append(final_g.reshape(1, d))
    return pl.pallas_call(
        body,
        grid=(bsz // nb, t // tt),
        in_specs=in_specs,
        out_specs=xspec,
        out_shape=jax.ShapeDtypeStruct(x.shape, x.dtype),
        compiler_params=pltpu.CompilerParams(
            dimension_semantics=("arbitrary", "arbitrary"),
            vmem_limit_bytes=V7X_VMEM_LIMIT_BYTES),
        name="ffn_final" if final else "ffn",
    )(*args)


def _mix_body(x_ref, mod_ref, h0_ref, lbuf0_ref, cbuf0_ref,
              mixg_ref, win_ref, lcw_ref, lcb_ref, wgate_ref, bgate_ref, lam_ref,
              cmw_ref, cmb_ref, lng_ref, lnb_ref, ogl_ref, ogc_ref, wout_ref,
              o_ref, ht_ref, ltail_ref, ctail_ref,
              lx_s, cu_s, h_s, hseq_s):
    t = pl.program_id(1)
    nb, tt, d = x_ref.shape
    dl = lx_s.shape[-1]
    dc = cu_s.shape[-1]
    hl = lbuf0_ref.shape[1]
    hc = cbuf0_ref.shape[1]
    pl_rows = lx_s.shape[1] - tt
    pc_rows = cu_s.shape[1] - tt
    rows = nb * tt
    sub = V7X_SUBLANES

    @pl.when(t == 0)
    def _load_state():
        lx_s[:, pl_rows - hl:pl_rows, :] = lbuf0_ref[...]
        cu_s[:, pc_rows - hc:pc_rows, :] = cbuf0_ref[...]
        h_s[...] = h0_ref[...]

    x = x_ref[...]
    sh = mod_ref[:, 3:4, :]
    sc = mod_ref[:, 4:5, :]
    gt = mod_ref[:, 5:6, :]
    h = _rms(x, mixg_ref[...]) * (1.0 + sc) + sh
    z = _dot(h.reshape(rows, d).astype(BF16), win_ref[...])
    z_lx = z[:, :dl]
    z_lg = z[:, dl:2 * dl]
    z_cv = z[:, 2 * dl:2 * dl + dc]
    z_cg = z[:, 2 * dl + dc:]

    lx_s[:, pl_rows:, :] = z_lx.reshape(nb, tt, dl)
    y = lcb_ref[...]
    for k in range(hl + 1):
        lo = pl_rows - hl + k
        y = y + lcw_ref[k:k + 1, :] * lx_s[:, lo:lo + tt, :]
    y = y.reshape(rows, dl)
    gates = _dot(y.astype(BF16), wgate_ref[...]) + bgate_ref[...]
    r = jax.nn.sigmoid(gates[:, :dl])
    i = jax.nn.sigmoid(gates[:, dl:])
    log_a = -LRU_C * r * jax.nn.softplus(-lam_ref[...])
    a = jnp.exp(log_a)
    b = jnp.sqrt(-jnp.tanh(log_a) * (a * a + 1.0)) * (i * y)

    groups = rows // sub
    a3 = a.reshape(groups, sub, dl)
    b3 = b.reshape(groups, sub, dl)
    row = jax.lax.broadcasted_iota(jnp.int32, (groups, sub, dl), 1)
    k = 1
    while k < sub:
        keep = row >= k
        a_prev = jnp.where(keep, pltpu.roll(a3, k, 1), 1.0)
        b_prev = jnp.where(keep, pltpu.roll(b3, k, 1), 0.0)
        b3 = a3 * b_prev + b3
        a3 = a3 * a_prev
        k *= 2
    gps = tt // sub
    for n in range(nb):
        carry = h_s[n]
        for gi in range(gps):
            g_idx = n * gps + gi
            hg = a3[g_idx] * carry + b3[g_idx]
            hseq_s[g_idx * sub:(g_idx + 1) * sub, :] = hg
            carry = hg[sub - 1:sub, :]
        h_s[n] = carry
    lru_out = hseq_s[...] * jax.nn.gelu(z_lg)
    n_lru = _rms(lru_out, ogl_ref[...])

    u = z_cv * jax.nn.sigmoid(z_cg)
    cu_s[:, pc_rows:, :] = u.reshape(nb, tt, dc)
    v = cmb_ref[...]
    for k in range(hc + 1):
        lo = pc_rows - hc + k
        v = v + cmw_ref[k:k + 1, :] * cu_s[:, lo:lo + tt, :]
    v = v.reshape(rows, dc)
    mu = jnp.mean(v, axis=-1, keepdims=True)
    vc = v - mu
    vn = vc * jax.lax.rsqrt(jnp.mean(vc * vc, axis=-1, keepdims=True) + EPS)
    vn = vn * lng_ref[...] + lnb_ref[...]
    vs = vn * jax.nn.sigmoid(vn)
    n_cv = _rms(vs, ogc_ref[...])

    merged = _dot(jnp.concatenate([n_lru, n_cv], axis=-1).astype(BF16), wout_ref[...])
    o_ref[...] = x + gt * merged.reshape(nb, tt, d)

    lx_s[:, :pl_rows, :] = lx_s[:, tt:tt + pl_rows, :]
    cu_s[:, :pc_rows, :] = cu_s[:, tt:tt + pc_rows, :]
    ht_ref[...] = h_s[...]
    ltail_ref[...] = lx_s[:, pl_rows - hl:pl_rows, :]
    ctail_ref[...] = cu_s[:, pc_rows - hc:pc_rows, :]


def _mixer(x, mod, h0, lbuf, cbuf, w, *, nb, tt):
    bsz, t, d = x.shape
    dl = h0.shape[-1]
    dc = cbuf.shape[-1]
    hl = lbuf.shape[1]
    hc = cbuf.shape[1]
    pl_rows = _round_up(hl, V7X_SUBLANES)
    pc_rows = _round_up(hc, V7X_SUBLANES)
    assert tt % V7X_SUBLANES == 0 and tt >= pc_rows and tt >= pl_rows and t % tt == 0

    def seq_spec(r, c):
        return pl.BlockSpec((nb, r, c), lambda b, i: (b, 0, 0))

    consts = [w["mix_g"], w["w_in"], w["lru_conv_w"], w["lru_conv_b"], w["w_gate"], w["b_gate"],
              w["lru_lambda"], w["cm_dw_w"], w["cm_dw_b"], w["cm_ln_g"], w["cm_ln_b"],
              w["out_g_lru"], w["out_g_conv"], w["w_out"]]
    xspec = pl.BlockSpec((nb, tt, d), lambda b, i: (b, i, 0))
    return pl.pallas_call(
        _mix_body,
        grid=(bsz // nb, t // tt),
        in_specs=[xspec, seq_spec(N_MOD, d), seq_spec(1, dl), seq_spec(hl, dl), seq_spec(hc, dc)]
        + [_const_spec(c.shape) for c in consts],
        out_specs=[xspec, seq_spec(1, dl), seq_spec(hl, dl), seq_spec(hc, dc)],
        out_shape=[
            jax.ShapeDtypeStruct(x.shape, F32),
            jax.ShapeDtypeStruct((bsz, 1, dl), F32),
            jax.ShapeDtypeStruct((bsz, hl, dl), F32),
            jax.ShapeDtypeStruct((bsz, hc, dc), F32),
        ],
        scratch_shapes=[
            pltpu.VMEM((nb, pl_rows + tt, dl), F32),
            pltpu.VMEM((nb, pc_rows + tt, dc), F32),
            pltpu.VMEM((nb, 1, dl), F32),
            pltpu.VMEM((nb * tt, dl), F32),
        ],
        compiler_params=pltpu.CompilerParams(
            dimension_semantics=("arbitrary", "arbitrary"),
            vmem_limit_bytes=V7X_VMEM_LIMIT_BYTES),
        name="mixer",
    )(x, mod, h0.reshape(bsz, 1, dl), lbuf, cbuf, *consts)


def _block_diag(w):
    heads, di, dj = w.shape
    eye = jnp.eye(heads, dtype=w.dtype)
    return jnp.einsum("hij,hk->hikj", w, eye).reshape(heads * di, heads * dj)


def _tiles(bsz, t):
    ffn_rows, mix_rows = 512, 256
    if t >= ffn_rows:
        return (1, ffn_rows), (1, mix_rows)
    nb_ffn = max(1, min(bsz, ffn_rows // t))
    nb_mix = max(1, min(bsz, mix_rows // t))
    return (nb_ffn, t), (nb_mix, t)


def _stream_layer(x, mod, h0, lbuf, cbuf, w, final_g):
    (nb_f, tt_f), (nb_m, tt_m) = _tiles(x.shape[0], x.shape[1])
    x = _ffn(x, mod, w["ffn1_g"], w["ffn1_wg"], w["ffn1_wu"], w["ffn1_wd"], None,
             mod_base=0, nb=nb_f, tt=tt_f)
    x, h_last, ltail, ctail = _mixer(x, mod, h0, lbuf, cbuf, w, nb=nb_m, tt=tt_m)
    x = _ffn(x, mod, w["ffn2_g"], w["ffn2_wg"], w["ffn2_wu"], w["ffn2_wd"], final_g,
             mod_base=6, nb=nb_f, tt=tt_f)
    return x, h_last[:, 0, :], ltail, ctail


def kernel(x_prompt, x_sample, state_lru_h, state_lru_conv, state_cm_conv, c_prompt, c_sample, w_ada, b_ada, ffn1_g, ffn1_wg, ffn1_wu, ffn1_wd, mix_g, w_in, lru_conv_w, lru_conv_b, lru_wa, lru_ba, lru_wx, lru_bx, lru_lambda, cm_dw_w, cm_dw_b, cm_ln_g, cm_ln_b, out_g_lru, out_g_conv, w_out, ffn2_g, ffn2_wg, ffn2_wu, ffn2_wd, final_g):
    depth = w_ada.shape[0]
    bp = x_prompt.shape[0]
    d = x_prompt.shape[-1]
    dl = lru_lambda.shape[-1]
    dc = cm_dw_b.shape[-1]
    hl = lru_conv_w.shape[1] - 1
    hc = cm_dw_w.shape[1] - 1
    xp, xs = x_prompt, x_sample
    outs = [[] for _ in range(6)]
    for l in range(depth):
        row = lambda v: v[l].reshape(1, -1)
        w = {
            "ffn1_g": ffn1_g[l], "ffn1_wg": ffn1_wg[l].astype(BF16),
            "ffn1_wu": ffn1_wu[l].astype(BF16), "ffn1_wd": ffn1_wd[l].astype(BF16),
            "ffn2_g": ffn2_g[l], "ffn2_wg": ffn2_wg[l].astype(BF16),
            "ffn2_wu": ffn2_wu[l].astype(BF16), "ffn2_wd": ffn2_wd[l].astype(BF16),
            "mix_g": row(mix_g), "w_in": w_in[l].astype(BF16),
            "lru_conv_w": lru_conv_w[l], "lru_conv_b": row(lru_conv_b),
            "w_gate": jnp.concatenate(
                [_block_diag(lru_wa[l]), _block_diag(lru_wx[l])], axis=1).astype(BF16),
            "b_gate": jnp.concatenate([lru_ba[l], lru_bx[l]]).reshape(1, -1),
            "lru_lambda": row(lru_lambda),
            "cm_dw_w": cm_dw_w[l], "cm_dw_b": row(cm_dw_b),
            "cm_ln_g": row(cm_ln_g), "cm_ln_b": row(cm_ln_b),
            "out_g_lru": row(out_g_lru), "out_g_conv": row(out_g_conv),
            "w_out": w_out[l].astype(BF16),
        }
        last = l == depth - 1
        fg = final_g if last else None
        mod = _ada(jnp.concatenate([c_prompt, c_sample], axis=0), w_ada[l], b_ada[l])
        mod = mod.reshape(mod.shape[0], N_MOD, d)
        xp, hp, lcp, ccp = _stream_layer(
            xp, mod[:bp], jnp.zeros((bp, dl), F32), jnp.zeros((bp, hl, dl), F32),
            jnp.zeros((bp, hc, dc), F32), w, fg)
        xs, hs, lcs, ccs = _stream_layer(
            xs, mod[bp:], state_lru_h[l], state_lru_conv[l], state_cm_conv[l], w, fg)
        for lst, v in zip(outs, (hp, lcp, ccp, hs, lcs, ccs)):
            lst.append(v)
    if depth == 0:
        raise ValueError("at least one layer is required")
    return (xp, xs) + tuple(jnp.stack(v, axis=0) for v in outs)
```

```python
import functools

import jax
import jax.numpy as jnp
from jax.experimental import pallas as pl
from jax.experimental.pallas import tpu as pltpu

EPS = 1e-6
LRU_C = 8.0
N_MOD = 9

V7X_SUBLANES = 8
V7X_LANES = 128
V7X_MXU_DIM = 256
V7X_VMEM_LIMIT_BYTES = 56 * 1024 * 1024

GROUP = V7X_SUBLANES

F32 = jnp.float32
BF16 = jnp.bfloat16


def _dot(a, b):
    return jnp.dot(a, b, preferred_element_type=F32)


def _rms(xf, g):
    return xf * jax.lax.rsqrt(jnp.mean(xf * xf, axis=-1, keepdims=True) + EPS) * g


def _const_spec(shape):
    zeros = (0,) * len(shape)
    return pl.BlockSpec(shape, lambda *_: zeros, pipeline_mode=pl.Buffered(1))


def _params(n_grid):
    return pltpu.CompilerParams(
        dimension_semantics=("arbitrary",) * n_grid, vmem_limit_bytes=V7X_VMEM_LIMIT_BYTES)


def _ada_body(c_ref, w_ref, b_ref, o_ref):
    c = c_ref[...]
    s = (c * jax.nn.sigmoid(c)).astype(BF16)
    o_ref[0] = _dot(s, w_ref[...].astype(BF16)) + b_ref[...]


def _ada(c, w_ada, b_ada):
    bsz, d = c.shape
    n_mod = w_ada.shape[1] // d
    return pl.pallas_call(
        _ada_body,
        grid=(n_mod,),
        in_specs=[
            pl.BlockSpec((bsz, d), lambda j: (0, 0)),
            pl.BlockSpec((d, d), lambda j: (0, j)),
            pl.BlockSpec((1, d), lambda j: (0, j)),
        ],
        out_specs=pl.BlockSpec((1, bsz, d), lambda j: (j, 0, 0)),
        out_shape=jax.ShapeDtypeStruct((n_mod, bsz, d), F32),
        compiler_params=_params(1),
        name="ada_mod",
    )(c, w_ada, b_ada.reshape(1, -1))


def _ffn_chunks(d_ff, width):
    out, lo = [], 0
    while lo < d_ff:
        sz = min(width, d_ff - lo)
        out.append((lo, sz))
        lo += sz
    return tuple(out)


def _swiglu(hb, wg_ref, wu_ref, wd_ref, chunks):
    acc = None
    for lo, sz in chunks:
        g = _dot(hb, wg_ref[:, lo:lo + sz])
        u = _dot(hb, wu_ref[:, lo:lo + sz])
        a = (g * jax.nn.sigmoid(g) * u).astype(BF16)
        p = _dot(a, wd_ref[lo:lo + sz, :])
        acc = p if acc is None else acc + p
    return acc


def _ffn_in_body(x_ref, mod_ref, g_ref, wg_ref, wu_ref, wd_ref, o_ref, *, mod_base, chunks):
    nb, tq, d = x_ref.shape
    hs = []
    for b in range(nb):
        sh = mod_ref[mod_base, b:b + 1, :]
        sc = mod_ref[mod_base + 1, b:b + 1, :]
        hs.append((_rms(x_ref[b], g_ref[...]) * (1.0 + sc) + sh).astype(BF16))
    acc = _swiglu(jnp.concatenate(hs, axis=0), wg_ref, wu_ref, wd_ref, chunks)
    for b in range(nb):
        gt = mod_ref[mod_base + 2, b:b + 1, :]
        y = x_ref[b] + (0.5 * gt) * acc[b * tq:(b + 1) * tq, :]
        for s in range(d // V7X_LANES):
            o_ref[0, s, pl.ds(b, tq, stride=nb), :] = y[:, s * V7X_LANES:(s + 1) * V7X_LANES]


def _ffn_out_body(*refs, mod_base, chunks, final):
    if final:
        x_ref, mod_ref, g_ref, wg_ref, wu_ref, wd_ref, fg_ref, o_ref, y_s = refs
    else:
        x_ref, mod_ref, g_ref, wg_ref, wu_ref, wd_ref, o_ref, y_s = refs
    nb, tq, d = o_ref.shape
    rows = nb * tq
    x = x_ref[0].reshape(tq, nb, d)
    sh = mod_ref[mod_base][None]
    sc = mod_ref[mod_base + 1][None]
    gt = mod_ref[mod_base + 2][None]
    h = _rms(x, g_ref[...]) * (1.0 + sc) + sh
    acc = _swiglu(h.reshape(rows, d).astype(BF16), wg_ref, wu_ref, wd_ref, chunks)
    y = x + (0.5 * gt) * acc.reshape(tq, nb, d)
    if final:
        y = _rms(y, fg_ref[...])
    y = y.reshape(rows, d)
    n_slab = d // V7X_LANES
    for s in range(n_slab):
        y_s[s] = y[:, s * V7X_LANES:(s + 1) * V7X_LANES]
    for b in range(nb):
        o_ref[b] = jnp.concatenate(
            [y_s[s, pl.ds(b, tq, stride=nb), :] for s in range(n_slab)], axis=-1)


def _ffn_in(x, mod, g, wg, wu, wd, *, mod_base, tq):
    bsz, t, d = x.shape
    d_ff = wg.shape[1]
    n_grp, n_slab = bsz // GROUP, d // V7X_LANES
    body = functools.partial(
        _ffn_in_body, mod_base=mod_base, chunks=_ffn_chunks(d_ff, 3 * V7X_MXU_DIM))
    return pl.pallas_call(
        body,
        grid=(n_grp, t // tq),
        in_specs=[
            pl.BlockSpec((GROUP, tq, d), lambda b, i: (b, i, 0)),
            pl.BlockSpec((N_MOD, GROUP, d), lambda b, i: (0, b, 0)),
            _const_spec((1, d)),
            _const_spec((d, d_ff)),
            _const_spec((d, d_ff)),
            _const_spec((d_ff, d)),
        ],
        out_specs=pl.BlockSpec((1, n_slab, tq * GROUP, V7X_LANES), lambda b, i: (b, 0, i, 0)),
        out_shape=jax.ShapeDtypeStruct((n_grp, n_slab, t * GROUP, V7X_LANES), F32),
        compiler_params=_params(2),
        name="ffn_in",
    )(x, mod, g.reshape(1, d), wg, wu, wd)


def _ffn_out(x, mod, g, wg, wu, wd, final_g, *, mod_base, tq):
    n_grp, rows_all, d = x.shape
    t = rows_all // GROUP
    d_ff = wg.shape[1]
    final = final_g is not None
    body = functools.partial(
        _ffn_out_body, mod_base=mod_base, chunks=_ffn_chunks(d_ff, 3 * V7X_MXU_DIM), final=final)
    in_specs = [
        pl.BlockSpec((1, tq * GROUP, d), lambda b, i: (b, i, 0)),
        pl.BlockSpec((N_MOD, GROUP, d), lambda b, i: (0, b, 0)),
        _const_spec((1, d)),
        _const_spec((d, d_ff)),
        _const_spec((d, d_ff)),
        _const_spec((d_ff, d)),
    ]
    args = [x, mod, g.reshape(1, d), wg, wu, wd]
    if final:
        in_specs.append(_const_spec((1, d)))
        args.append(final_g.reshape(1, d))
    return pl.pallas_call(
        body,
        grid=(n_grp, t // tq),
        in_specs=in_specs,
        out_specs=pl.BlockSpec((GROUP, tq, d), lambda b, i: (b, i, 0)),
        out_shape=jax.ShapeDtypeStruct((n_grp * GROUP, t, d), F32),
        scratch_shapes=[pltpu.VMEM((d // V7X_LANES, tq * GROUP, V7X_LANES), F32)],
        compiler_params=_params(2),
        name="ffn_out_final" if final else "ffn_out",
    )(*args)


def _dwconv(buf_ref, w_ref, b_ref, rows, n_taps):
    width = buf_ref.shape[-1]
    n_out = rows // GROUP
    col_blocks = []
    for c0 in range(0, width, V7X_LANES):
        lanes = slice(c0, c0 + V7X_LANES)
        taps = [jnp.broadcast_to(w_ref[k:k + 1, lanes], (GROUP, V7X_LANES)) for k in range(n_taps)]
        bias = jnp.broadcast_to(b_ref[:, lanes], (GROUP, V7X_LANES))
        tiles = [buf_ref[j * GROUP:(j + 1) * GROUP, lanes] for j in range(n_out + n_taps - 1)]
        outs = []
        for j in range(n_out):
            acc = bias
            for k in range(n_taps):
                acc = acc + taps[k] * tiles[j + k]
            outs.append(acc)
        col_blocks.append(jnp.concatenate(outs, axis=0))
    return jnp.concatenate(col_blocks, axis=-1)


def _mix_body(x_ref, mod_ref, h0_ref, lbuf0_ref, cbuf0_ref,
              mixg_ref, win_ref, lcw_ref, lcb_ref, wgate_ref, bgate_ref, lam_ref,
              cmw_ref, cmb_ref, lng_ref, lnb_ref, ogl_ref, ogc_ref, wout_ref,
              o_ref, ht_ref, ltail_ref, ctail_ref,
              lx_s, cu_s, h_s, hseq_s):
    t_idx = pl.program_id(1)
    rows, d = o_ref.shape[1:]
    tt = rows // GROUP
    dl = lx_s.shape[-1]
    dc = cu_s.shape[-1]
    hl_rows = lbuf0_ref.shape[1]
    hc_rows = cbuf0_ref.shape[1]
    n_gate_grp, gsz = wgate_ref.shape[0], wgate_ref.shape[1]

    @pl.when(t_idx == 0)
    def _load_state():
        lx_s[:hl_rows, :] = lbuf0_ref[0]
        cu_s[:hc_rows, :] = cbuf0_ref[0]
        h_s[...] = h0_ref[0]

    x = jnp.concatenate([x_ref[0, s] for s in range(x_ref.shape[1])], axis=-1)
    x3 = x.reshape(tt, GROUP, d)
    sh = mod_ref[3][None]
    sc = mod_ref[4][None]
    gt = mod_ref[5][None]
    h = _rms(x3, mixg_ref[...]) * (1.0 + sc) + sh
    z = _dot(h.reshape(rows, d).astype(BF16), win_ref[...])
    z_lg = z[:, dl:2 * dl]
    z_cv = z[:, 2 * dl:2 * dl + dc]
    z_cg = z[:, 2 * dl + dc:]

    lx_s[hl_rows:, :] = z[:, :dl]
    y = _dwconv(lx_s, lcw_ref, lcb_ref, rows, hl_rows // GROUP + 1)
    yb = y.astype(BF16)
    r_parts, i_parts = [], []
    for gi in range(n_gate_grp):
        gates = _dot(yb[:, gi * gsz:(gi + 1) * gsz], wgate_ref[gi]) + bgate_ref[gi]
        r_parts.append(jax.nn.sigmoid(gates[:, :gsz]))
        i_parts.append(jax.nn.sigmoid(gates[:, gsz:]))
    r = jnp.concatenate(r_parts, axis=-1)
    i = jnp.concatenate(i_parts, axis=-1)
    log_a = -LRU_C * r * jax.nn.softplus(-lam_ref[...])
    a = jnp.exp(log_a)
    b = jnp.sqrt(-jnp.tanh(log_a) * (a * a + 1.0)) * (i * y)

    carry = h_s[...]
    for j in range(tt):
        rs = slice(j * GROUP, (j + 1) * GROUP)
        carry = a[rs, :] * carry + b[rs, :]
        hseq_s[rs, :] = carry
    h_s[...] = carry
    lru_out = hseq_s[...] * jax.nn.gelu(z_lg)
    n_lru = _rms(lru_out, ogl_ref[...])

    cu_s[hc_rows:, :] = z_cv * jax.nn.sigmoid(z_cg)
    v = _dwconv(cu_s, cmw_ref, cmb_ref, rows, hc_rows // GROUP + 1)
    mu = jnp.mean(v, axis=-1, keepdims=True)
    vc = v - mu
    vn = vc * jax.lax.rsqrt(jnp.mean(vc * vc, axis=-1, keepdims=True) + EPS)
    vn = vn * lng_ref[...] + lnb_ref[...]
    vs = vn * jax.nn.sigmoid(vn)
    n_cv = _rms(vs, ogc_ref[...])

    merged = _dot(jnp.concatenate([n_lru, n_cv], axis=-1).astype(BF16), wout_ref[...])
    o_ref[0] = (x3 + gt * merged.reshape(tt, GROUP, d)).reshape(rows, d)

    lx_s[:hl_rows, :] = lx_s[rows:rows + hl_rows, :]
    cu_s[:hc_rows, :] = cu_s[rows:rows + hc_rows, :]
    ht_ref[0] = h_s[...]
    ltail_ref[0] = lx_s[:hl_rows, :]
    ctail_ref[0] = cu_s[:hc_rows, :]


def _mixer(x, mod, h0, lbuf, cbuf, w, *, tt):
    n_grp, n_slab, rows_all, _ = x.shape
    d = n_slab * V7X_LANES
    t = rows_all // GROUP
    rows = tt * GROUP
    dl, dc = h0.shape[-1], cbuf.shape[-1]
    hl_rows, hc_rows = lbuf.shape[1], cbuf.shape[1]
    assert t % tt == 0 and rows >= hc_rows and rows >= hl_rows

    def grp_spec(r, c):
        return pl.BlockSpec((1, r, c), lambda b, i: (b, 0, 0))

    consts = [w["mix_g"], w["w_in"], w["lru_conv_w"], w["lru_conv_b"], w["w_gate"], w["b_gate"],
              w["lru_lambda"], w["cm_dw_w"], w["cm_dw_b"], w["cm_ln_g"], w["cm_ln_b"],
              w["out_g_lru"], w["out_g_conv"], w["w_out"]]
    return pl.pallas_call(
        _mix_body,
        grid=(n_grp, t // tt),
        in_specs=[
            pl.BlockSpec((1, n_slab, rows, V7X_LANES), lambda b, i: (b, 0, i, 0)),
            pl.BlockSpec((N_MOD, GROUP, d), lambda b, i: (0, b, 0)),
            grp_spec(GROUP, dl), grp_spec(hl_rows, dl), grp_spec(hc_rows, dc),
        ] + [_const_spec(c.shape) for c in consts],
        out_specs=[
            pl.BlockSpec((1, rows, d), lambda b, i: (b, i, 0)),
            grp_spec(GROUP, dl), grp_spec(hl_rows, dl), grp_spec(hc_rows, dc),
        ],
        out_shape=[
            jax.ShapeDtypeStruct((n_grp, rows_all, d), F32),
            jax.ShapeDtypeStruct((n_grp, GROUP, dl), F32),
            jax.ShapeDtypeStruct((n_grp, hl_rows, dl), F32),
            jax.ShapeDtypeStruct((n_grp, hc_rows, dc), F32),
        ],
        scratch_shapes=[
            pltpu.VMEM((hl_rows + rows, dl), F32),
            pltpu.VMEM((hc_rows + rows, dc), F32),
            pltpu.VMEM((GROUP, dl), F32),
            pltpu.VMEM((rows, dl), F32),
        ],
        compiler_params=_params(2),
        name="mixer",
    )(x, mod, h0, lbuf, cbuf, *consts)


def _block_diag(w):
    heads, di, dj = w.shape
    eye = jnp.eye(heads, dtype=w.dtype)
    return jnp.einsum("hij,hk->hikj", w, eye).reshape(heads * di, heads * dj)


def _gate_weights(wa, wx, ba, bx):
    heads, hd, _ = wa.shape
    per = max(1, min(heads, V7X_MXU_DIM // hd))
    while heads % per:
        per -= 1
    n_grp, gsz = heads // per, per * hd
    ws, bs = [], []
    for gi in range(n_grp):
        hsl = slice(gi * per, (gi + 1) * per)
        csl = slice(gi * gsz, (gi + 1) * gsz)
        ws.append(jnp.concatenate([_block_diag(wa[hsl]), _block_diag(wx[hsl])], axis=1))
        bs.append(jnp.concatenate([ba[csl], bx[csl]]).reshape(1, -1))
    return jnp.stack(ws).astype(BF16), jnp.stack(bs)


def _to_groups(state):
    bsz, r, c = state.shape
    return state.reshape(bsz // GROUP, GROUP, r, c).transpose(0, 2, 1, 3).reshape(
        bsz // GROUP, r * GROUP, c)


def _from_groups(state, r):
    n_grp, _, c = state.shape
    return state.reshape(n_grp, r, GROUP, c).transpose(0, 2, 1, 3).reshape(n_grp * GROUP, r, c)


def _stream_layer(x, mod, h0, lbuf, cbuf, w, final_g):
    bsz, t, _ = x.shape
    assert bsz % GROUP == 0
    tq = min(t, 64)
    tt = min(t, 32)
    hl, hc = lbuf.shape[1], cbuf.shape[1]
    x = _ffn_in(x, mod, w["ffn1_g"], w["ffn1_wg"], w["ffn1_wu"], w["ffn1_wd"], mod_base=0, tq=tq)
    x, h_last, ltail, ctail = _mixer(
        x, mod, h0.reshape(bsz // GROUP, GROUP, -1), _to_groups(lbuf), _to_groups(cbuf), w, tt=tt)
    x = _ffn_out(x, mod, w["ffn2_g"], w["ffn2_wg"], w["ffn2_wu"], w["ffn2_wd"], final_g,
                 mod_base=6, tq=tq)
    return x, h_last.reshape(bsz, -1), _from_groups(ltail, hl), _from_groups(ctail, hc)


def kernel(x_prompt, x_sample, state_lru_h, state_lru_conv, state_cm_conv, c_prompt, c_sample, w_ada, b_ada, ffn1_g, ffn1_wg, ffn1_wu, ffn1_wd, mix_g, w_in, lru_conv_w, lru_conv_b, lru_wa, lru_ba, lru_wx, lru_bx, lru_lambda, cm_dw_w, cm_dw_b, cm_ln_g, cm_ln_b, out_g_lru, out_g_conv, w_out, ffn2_g, ffn2_wg, ffn2_wu, ffn2_wd, final_g):
    depth = w_ada.shape[0]
    if depth == 0:
        raise ValueError("at least one layer is required")
    bp = x_prompt.shape[0]
    dl = lru_lambda.shape[-1]
    dc = cm_dw_b.shape[-1]
    hl = lru_conv_w.shape[1] - 1
    hc = cm_dw_w.shape[1] - 1
    xp, xs = x_prompt, x_sample
    outs = [[] for _ in range(6)]
    for l in range(depth):
        row = lambda v: v[l].reshape(1, -1)
        w_gate, b_gate = _gate_weights(lru_wa[l], lru_wx[l], lru_ba[l], lru_bx[l])
        w = {
            "ffn1_g": ffn1_g[l], "ffn1_wg": ffn1_wg[l].astype(BF16),
            "ffn1_wu": ffn1_wu[l].astype(BF16), "ffn1_wd": ffn1_wd[l].astype(BF16),
            "ffn2_g": ffn2_g[l], "ffn2_wg": ffn2_wg[l].astype(BF16),
            "ffn2_wu": ffn2_wu[l].astype(BF16), "ffn2_wd": ffn2_wd[l].astype(BF16),
            "mix_g": row(mix_g), "w_in": w_in[l].astype(BF16),
            "lru_conv_w": lru_conv_w[l], "lru_conv_b": row(lru_conv_b),
            "w_gate": w_gate, "b_gate": b_gate,
            "lru_lambda": row(lru_lambda),
            "cm_dw_w": cm_dw_w[l], "cm_dw_b": row(cm_dw_b),
            "cm_ln_g": row(cm_ln_g), "cm_ln_b": row(cm_ln_b),
            "out_g_lru": row(out_g_lru), "out_g_conv": row(out_g_conv),
            "w_out": w_out[l].astype(BF16),
        }
        fg = final_g if l == depth - 1 else None
        mod = _ada(jnp.concatenate([c_prompt, c_sample], axis=0), w_ada[l], b_ada[l])
        xp, hp, lcp, ccp = _stream_layer(
            xp, mod[:, :bp], jnp.zeros((bp, dl), F32), jnp.zeros((bp, hl, dl), F32),
            jnp.zeros((bp, hc, dc), F32), w, fg)
        xs, hs, lcs, ccs = _stream_layer(
            xs, mod[:, bp:], state_lru_h[l], state_lru_conv[l], state_cm_conv[l], w, fg)
        for lst, v in zip(outs, (hp, lcp, ccp, hs, lcs, ccs)):
            lst.append(v)
    return (xp, xs) + tuple(jnp.stack(v, axis=0) for v in outs)
```

```python
import functools

import jax
import jax.numpy as jnp
from jax.experimental import pallas as pl
from jax.experimental.pallas import tpu as pltpu

EPS = 1e-6
LRU_C = 8.0
N_MOD = 9

V7X_SUBLANES = 8
V7X_LANES = 128
V7X_MXU_DIM = 256
V7X_VMEM_LIMIT_BYTES = 56 * 1024 * 1024

GROUP = V7X_SUBLANES

F32 = jnp.float32
BF16 = jnp.bfloat16


def _dot(a, b):
    return jnp.dot(a, b, preferred_element_type=F32)


def _rms(xf, g):
    return xf * jax.lax.rsqrt(jnp.mean(xf * xf, axis=-1, keepdims=True) + EPS) * g


def _const_spec(shape):
    zeros = (0,) * len(shape)
    return pl.BlockSpec(shape, lambda *_: zeros, pipeline_mode=pl.Buffered(1))


def _params(n_grid):
    return pltpu.CompilerParams(
        dimension_semantics=("arbitrary",) * n_grid, vmem_limit_bytes=V7X_VMEM_LIMIT_BYTES)


def _ada_body(c_ref, w_ref, b_ref, o_ref):
    c = c_ref[...]
    s = (c * jax.nn.sigmoid(c)).astype(BF16)
    o_ref[0] = _dot(s, w_ref[...].astype(BF16)) + b_ref[...]


def _ada(c, w_ada, b_ada):
    bsz, d = c.shape
    n_mod = w_ada.shape[1] // d
    return pl.pallas_call(
        _ada_body,
        grid=(n_mod,),
        in_specs=[
            pl.BlockSpec((bsz, d), lambda j: (0, 0)),
            pl.BlockSpec((d, d), lambda j: (0, j)),
            pl.BlockSpec((1, d), lambda j: (0, j)),
        ],
        out_specs=pl.BlockSpec((1, bsz, d), lambda j: (j, 0, 0)),
        out_shape=jax.ShapeDtypeStruct((n_mod, bsz, d), F32),
        compiler_params=_params(1),
        name="ada_mod",
    )(c, w_ada, b_ada.reshape(1, -1))


def _ffn_chunks(d_ff, width):
    out, lo = [], 0
    while lo < d_ff:
        sz = min(width, d_ff - lo)
        out.append((lo, sz))
        lo += sz
    return tuple(out)


def _swiglu(hb, wg_ref, wu_ref, wd_ref, chunks):
    acc = None
    for lo, sz in chunks:
        g = _dot(hb, wg_ref[:, lo:lo + sz])
        u = _dot(hb, wu_ref[:, lo:lo + sz])
        a = (g * jax.nn.sigmoid(g) * u).astype(BF16)
        p = _dot(a, wd_ref[lo:lo + sz, :])
        acc = p if acc is None else acc + p
    return acc


def _ffn_in_body(x_ref, mod_ref, g_ref, wg_ref, wu_ref, wd_ref, o_ref, *, mod_base, chunks):
    nb, tq, d = x_ref.shape
    hs = []
    for b in range(nb):
        sh = mod_ref[mod_base, b:b + 1, :]
        sc = mod_ref[mod_base + 1, b:b + 1, :]
        hs.append((_rms(x_ref[b], g_ref[...]) * (1.0 + sc) + sh).astype(BF16))
    acc = _swiglu(jnp.concatenate(hs, axis=0), wg_ref, wu_ref, wd_ref, chunks)
    for b in range(nb):
        gt = mod_ref[mod_base + 2, b:b + 1, :]
        y = x_ref[b] + (0.5 * gt) * acc[b * tq:(b + 1) * tq, :]
        for s in range(d // V7X_LANES):
            o_ref[0, s, pl.ds(b, tq, stride=nb), :] = y[:, s * V7X_LANES:(s + 1) * V7X_LANES]


def _ffn_out_body(*refs, mod_base, chunks, final):
    if final:
        x_ref, mod_ref, g_ref, wg_ref, wu_ref, wd_ref, fg_ref, o_ref, y_s = refs
    else:
        x_ref, mod_ref, g_ref, wg_ref, wu_ref, wd_ref, o_ref, y_s = refs
    nb, tq, d = o_ref.shape
    rows = nb * tq
    x = jnp.concatenate([x_ref[0, s] for s in range(x_ref.shape[1])], axis=-1).reshape(tq, nb, d)
    sh = mod_ref[mod_base][None]
    sc = mod_ref[mod_base + 1][None]
    gt = mod_ref[mod_base + 2][None]
    h = _rms(x, g_ref[...]) * (1.0 + sc) + sh
    acc = _swiglu(h.reshape(rows, d).astype(BF16), wg_ref, wu_ref, wd_ref, chunks)
    y = x + (0.5 * gt) * acc.reshape(tq, nb, d)
    if final:
        y = _rms(y, fg_ref[...])
    y = y.reshape(rows, d)
    n_slab = d // V7X_LANES
    for s in range(n_slab):
        y_s[s] = y[:, s * V7X_LANES:(s + 1) * V7X_LANES]
    for b in range(nb):
        o_ref[b] = jnp.concatenate(
            [y_s[s, pl.ds(b, tq, stride=nb), :] for s in range(n_slab)], axis=-1)


def _ffn_in(x, mod, g, wg, wu, wd, *, mod_base, tq):
    bsz, t, d = x.shape
    d_ff = wg.shape[1]
    n_grp, n_slab = bsz // GROUP, d // V7X_LANES
    body = functools.partial(
        _ffn_in_body, mod_base=mod_base, chunks=_ffn_chunks(d_ff, 3 * V7X_MXU_DIM))
    return pl.pallas_call(
        body,
        grid=(n_grp, t // tq),
        in_specs=[
            pl.BlockSpec((GROUP, tq, d), lambda b, i: (b, i, 0)),
            pl.BlockSpec((N_MOD, GROUP, d), lambda b, i: (0, b, 0)),
            _const_spec((1, d)),
            _const_spec((d, d_ff)),
            _const_spec((d, d_ff)),
            _const_spec((d_ff, d)),
        ],
        out_specs=pl.BlockSpec((1, n_slab, tq * GROUP, V7X_LANES), lambda b, i: (b, 0, i, 0)),
        out_shape=jax.ShapeDtypeStruct((n_grp, n_slab, t * GROUP, V7X_LANES), F32),
        compiler_params=_params(2),
        name="ffn_in",
    )(x, mod, g.reshape(1, d), wg, wu, wd)


def _ffn_out(x, mod, g, wg, wu, wd, final_g, *, mod_base, tq):
    n_grp, n_in_slab, rows_all, in_w = x.shape
    d = n_in_slab * in_w
    t = rows_all // GROUP
    d_ff = wg.shape[1]
    final = final_g is not None
    body = functools.partial(
        _ffn_out_body, mod_base=mod_base, chunks=_ffn_chunks(d_ff, 3 * V7X_MXU_DIM), final=final)
    in_specs = [
        pl.BlockSpec((1, n_in_slab, tq * GROUP, in_w), lambda b, i: (b, 0, i, 0)),
        pl.BlockSpec((N_MOD, GROUP, d), lambda b, i: (0, b, 0)),
        _const_spec((1, d)),
        _const_spec((d, d_ff)),
        _const_spec((d, d_ff)),
        _const_spec((d_ff, d)),
    ]
    args = [x, mod, g.reshape(1, d), wg, wu, wd]
    if final:
        in_specs.append(_const_spec((1, d)))
        args.append(final_g.reshape(1, d))
    return pl.pallas_call(
        body,
        grid=(n_grp, t // tq),
        in_specs=in_specs,
        out_specs=pl.BlockSpec((GROUP, tq, d), lambda b, i: (b, i, 0)),
        out_shape=jax.ShapeDtypeStruct((n_grp * GROUP, t, d), F32),
        scratch_shapes=[pltpu.VMEM((d // V7X_LANES, tq * GROUP, V7X_LANES), F32)],
        compiler_params=_params(2),
        name="ffn_out_final" if final else "ffn_out",
    )(*args)


MIX_ROWS = 256
CHUNK_ROWS = 32
N_CHUNKS = MIX_ROWS // CHUNK_ROWS


def _bcast(row):
    return jnp.broadcast_to(row, (GROUP, row.shape[-1]))


def _zero_after(tiles):
    bits = None
    for t in tiles:
        w = jax.lax.shift_right_logical(pltpu.bitcast(t, jnp.uint32), jnp.uint32(16))
        w = jax.lax.shift_right_logical(w, jnp.uint32(16))
        bits = w if bits is None else bits | w
    return pltpu.bitcast(bits, F32)


def _rows(ref, r0, n, lo, hi):
    width = ref.shape[-1]
    parts = []
    while lo < hi:
        slab, off = divmod(lo, width)
        take = min(hi - lo, width - off)
        parts.append(ref[slab, pl.ds(r0, n), off:off + take])
        lo += take
    return parts[0] if len(parts) == 1 else jnp.concatenate(parts, axis=-1)


def _core_chunk(r0, carry, z_r, m_w, lx_s, cu_s, wb_s,
                lcw_ref, lcb_ref, wgate_ref, bgate_ref, lam_ref,
                cmb_ref, lng_ref, lnb_ref, ogl_ref, ogc_ref):
    dl, dc = lx_s.shape[-1], cu_s.shape[-1]
    hl_t = (lx_s.shape[0] - MIX_ROWS) // GROUP
    hc_t = (cu_s.shape[0] - MIX_ROWS) // GROUP
    n_gate_grp, gsz = wgate_ref.shape[0], wgate_ref.shape[1]
    steps = CHUNK_ROWS // GROUP

    def tile_rows(j):
        return pl.ds(pl.multiple_of(r0 + j * GROUP, GROUP), GROUP)

    chunk = pl.ds(r0, CHUNK_ROWS)

    lx_s[pl.ds(pl.multiple_of(r0 + hl_t * GROUP, GROUP), CHUNK_ROWS), :] = _rows(
        z_r, r0, CHUNK_ROWS, 0, dl)
    cols = []
    for c0 in range(0, dl, V7X_LANES):
        lanes = slice(c0, c0 + V7X_LANES)
        taps = [_bcast(lcw_ref[k:k + 1, lanes]) for k in range(hl_t + 1)]
        bias = _bcast(lcb_ref[:, lanes])
        tiles = [lx_s[tile_rows(i), lanes] for i in range(steps + hl_t)]
        outs = []
        for r in range(steps):
            acc = bias
            for k in range(hl_t + 1):
                acc = acc + taps[k] * tiles[r + k]
            outs.append(acc)
        cols.append(jnp.concatenate(outs, axis=0))
    y = jnp.concatenate(cols, axis=-1)
    yb = y.astype(BF16)
    gates = [_dot(yb[:, gi * gsz:(gi + 1) * gsz], wgate_ref[gi]) + bgate_ref[gi]
             for gi in range(n_gate_grp)]
    g_r = jnp.concatenate([g[:, :gsz] for g in gates], axis=-1)
    g_i = jnp.concatenate([g[:, gsz:] for g in gates], axis=-1)
    log_a = jax.nn.sigmoid(g_r) * (-LRU_C * jax.nn.softplus(-lam_ref[...]))
    a = jnp.exp(log_a)
    b = jnp.sqrt(-jnp.tanh(log_a) * (a * a + 1.0)) * (jax.nn.sigmoid(g_i) * y)
    hs = []
    for j in range(steps):
        rs = slice(j * GROUP, (j + 1) * GROUP)
        carry = a[rs, :] * carry + b[rs, :]
        hs.append(carry)
    lru_out = jnp.concatenate(hs, axis=0) * jax.nn.gelu(_rows(z_r, r0, CHUNK_ROWS, dl, 2 * dl))
    m_w[chunk, 0:dl] = _rms(lru_out, ogl_ref[...]).astype(BF16)

    z_cv = _rows(z_r, r0, CHUNK_ROWS, 2 * dl, 2 * dl + dc)
    z_cg = _rows(z_r, r0, CHUNK_ROWS, 2 * dl + dc, 2 * dl + 2 * dc)
    cu_s[pl.ds(pl.multiple_of(r0 + hc_t * GROUP, GROUP), CHUNK_ROWS), :] = (
        z_cv * jax.nn.sigmoid(z_cg))
    cols = []
    prev = None
    for c0 in range(0, dc, V7X_LANES):
        lanes = slice(c0, c0 + V7X_LANES)
        bias = _bcast(cmb_ref[:, lanes])
        if prev is not None:
            bias = bias + _zero_after(prev)
        accs = [bias] * steps
        tiles = {}
        for k in range(hc_t + 1):
            tap = wb_s[k * GROUP:(k + 1) * GROUP, lanes]
            for r in range(steps):
                if r + k not in tiles:
                    tiles[r + k] = cu_s[tile_rows(r + k), lanes]
                accs[r] = accs[r] + tap * tiles[r + k]
        prev = accs
        cols.append(jnp.concatenate(accs, axis=0))
    v = jnp.concatenate(cols, axis=-1)
    mu = jnp.mean(v, axis=-1, keepdims=True)
    vc = v - mu
    vn = vc * jax.lax.rsqrt(jnp.mean(vc * vc, axis=-1, keepdims=True) + EPS)
    vn = vn * lng_ref[...] + lnb_ref[...]
    m_w[chunk, dl:dl + dc] = _rms(vn * jax.nn.sigmoid(vn), ogc_ref[...]).astype(BF16)
    return carry


def _mix_body(xm_ref, xc_ref, modm_ref, modc_ref, h0_ref, lbuf0_ref, cbuf0_ref,
              mixg_ref, win_ref, lcw_ref, lcb_ref, wgate_ref, bgate_ref, lam_ref,
              cmw_ref, cmb_ref, lng_ref, lnb_ref, ogl_ref, ogc_ref, wout_ref,
              o_ref, ht_ref, ltail_ref, ctail_ref,
              hb0_s, hb1_s, z0_s, z1_s, m0_s, m1_s, lx_s, cu_s, h_s, wb_s,
              *, n_tiles, tiles_per_grp):
    g = pl.program_id(0)
    core_valid = jnp.logical_and(g >= 2, g <= n_tiles + 1)
    t_core = jax.lax.rem(jnp.maximum(g - 2, 0), tiles_per_grp)
    n_lane_slab = xm_ref.shape[1]
    d = n_lane_slab * V7X_LANES
    hl_rows, hc_rows = lbuf0_ref.shape[1], cbuf0_ref.shape[1]
    out_w = o_ref.shape[-1]
    lane_per_out = out_w // V7X_LANES

    @pl.when(g == 0)
    def _first_step():
        hb1_s[...] = jnp.zeros_like(hb1_s)
        z0_s[...] = jnp.zeros_like(z0_s)
        m1_s[...] = jnp.zeros_like(m1_s)
        lx_s[...] = jnp.zeros_like(lx_s)
        cu_s[...] = jnp.zeros_like(cu_s)
        h_s[...] = jnp.zeros_like(h_s)
        for k in range(cmw_ref.shape[0]):
            wb_s[k * GROUP:(k + 1) * GROUP, :] = _bcast(cmw_ref[k:k + 1, :])

    @pl.when(jnp.logical_and(core_valid, t_core == 0))
    def _load_state():
        lx_s[:hl_rows, :] = lbuf0_ref[0]
        cu_s[:hc_rows, :] = cbuf0_ref[0]
        h_s[...] = h0_ref[0]

    def step(hb_w, hb_r, z_w, z_r, m_w, m_r):
        sh = modm_ref[3][None]
        sc1 = 1.0 + modm_ref[4][None]
        steps = CHUNK_ROWS // GROUP

        def body(q, carry):
            for sub in range(2):
                i = 2 * q + sub
                r0 = pl.multiple_of(i * CHUNK_ROWS, CHUNK_ROWS)
                if sub == 0:
                    z_w[q] = _dot(hb_r[...], win_ref[q])
                    merged = _dot(m_r[...], wout_ref[q])
                    x_res = jnp.concatenate(
                        [xc_ref[0, q * lane_per_out + k] for k in range(lane_per_out)], axis=-1)
                    gt = modc_ref[5, q][None]
                    o_ref[0, q] = (
                        x_res.reshape(MIX_ROWS // GROUP, GROUP, out_w)
                        + gt * merged.reshape(MIX_ROWS // GROUP, GROUP, out_w)
                    ).reshape(MIX_ROWS, out_w)
                x = jnp.concatenate(
                    [xm_ref[0, sl, pl.ds(r0, CHUNK_ROWS), :] for sl in range(n_lane_slab)], axis=-1)
                h = _rms(x.reshape(steps, GROUP, d), mixg_ref[...]) * sc1 + sh
                hb_w[pl.ds(r0, CHUNK_ROWS), :] = h.reshape(CHUNK_ROWS, d).astype(BF16)
                carry = _core_chunk(r0, carry, z_r, m_w, lx_s, cu_s, wb_s,
                                    lcw_ref, lcb_ref, wgate_ref, bgate_ref, lam_ref,
                                    cmb_ref, lng_ref, lnb_ref, ogl_ref, ogc_ref)
            return carry

        h_s[...] = jax.lax.fori_loop(0, N_CHUNKS // 2, body, h_s[...])
        lx_s[:hl_rows, :] = lx_s[MIX_ROWS:MIX_ROWS + hl_rows, :]
        cu_s[:hc_rows, :] = cu_s[MIX_ROWS:MIX_ROWS + hc_rows, :]

    @pl.when(jax.lax.rem(g, 2) == 0)
    def _even():
        step(hb0_s, hb1_s, z1_s, z0_s, m0_s, m1_s)

    @pl.when(jax.lax.rem(g, 2) == 1)
    def _odd():
        step(hb1_s, hb0_s, z0_s, z1_s, m1_s, m0_s)

    @pl.when(jnp.logical_and(core_valid, t_core == tiles_per_grp - 1))
    def _store_state():
        ht_ref[0] = h_s[...]
        ltail_ref[0] = lx_s[:hl_rows, :]
        ctail_ref[0] = cu_s[:hc_rows, :]


def _mixer(x, mod, h0, lbuf, cbuf, w):
    n_grp, n_lane_slab, rows_all, _ = x.shape
    d = n_lane_slab * V7X_LANES
    dl, dc = h0.shape[-1], cbuf.shape[-1]
    zw = 2 * dl + 2 * dc
    hl_rows, hc_rows = lbuf.shape[1], cbuf.shape[1]
    n_taps_c = w["cm_dw_w"].shape[0]
    in_w = zw // (N_CHUNKS // 2)
    out_w = d // (N_CHUNKS // 2)
    assert rows_all % MIX_ROWS == 0 and MIX_ROWS >= hc_rows and MIX_ROWS >= hl_rows
    assert in_w % V7X_LANES == 0 and out_w % V7X_LANES == 0
    nt = rows_all // MIX_ROWS
    n_tiles = n_grp * nt
    n_out_slab = d // out_w

    w_in = w["w_in"].reshape(d, N_CHUNKS // 2, in_w).transpose(1, 0, 2)
    w_out = w["w_out"].reshape(dl + dc, n_out_slab, out_w).transpose(1, 0, 2)
    mod_out = mod.reshape(N_MOD, -1, n_out_slab, out_w).transpose(0, 2, 1, 3)

    def tile_of(g, lag):
        return jnp.clip(g - lag, 0, n_tiles - 1)

    def x_spec(lag):
        return pl.BlockSpec(
            (1, n_lane_slab, MIX_ROWS, V7X_LANES),
            lambda g: (tile_of(g, lag) // nt, 0, tile_of(g, lag) % nt, 0))

    def grp_spec(r, c):
        return pl.BlockSpec((1, r, c), lambda g: (tile_of(g, 2) // nt, 0, 0))

    consts = [w["mix_g"], w_in, w["lru_conv_w"], w["lru_conv_b"], w["w_gate"], w["b_gate"],
              w["lru_lambda"], w["cm_dw_w"], w["cm_dw_b"], w["cm_ln_g"], w["cm_ln_b"],
              w["out_g_lru"], w["out_g_conv"], w_out]
    body = functools.partial(_mix_body, n_tiles=n_tiles, tiles_per_grp=nt)
    return pl.pallas_call(
        body,
        grid=(n_tiles + 3,),
        in_specs=[
            x_spec(0), x_spec(3),
            pl.BlockSpec((N_MOD, GROUP, d), lambda g: (0, tile_of(g, 0) // nt, 0)),
            pl.BlockSpec((N_MOD, n_out_slab, GROUP, out_w),
                         lambda g: (0, 0, tile_of(g, 3) // nt, 0)),
            grp_spec(GROUP, dl), grp_spec(hl_rows, dl), grp_spec(hc_rows, dc),
        ] + [_const_spec(c.shape) for c in consts],
        out_specs=[
            pl.BlockSpec((1, n_out_slab, MIX_ROWS, out_w),
                         lambda g: (tile_of(g, 3) // nt, 0, tile_of(g, 3) % nt, 0)),
            grp_spec(GROUP, dl), grp_spec(hl_rows, dl), grp_spec(hc_rows, dc),
        ],
        out_shape=[
            jax.ShapeDtypeStruct((n_grp, n_out_slab, rows_all, out_w), F32),
            jax.ShapeDtypeStruct((n_grp, GROUP, dl), F32),
            jax.ShapeDtypeStruct((n_grp, hl_rows, dl), F32),
            jax.ShapeDtypeStruct((n_grp, hc_rows, dc), F32),
        ],
        scratch_shapes=[
            pltpu.VMEM((MIX_ROWS, d), BF16),
            pltpu.VMEM((MIX_ROWS, d), BF16),
            pltpu.VMEM((N_CHUNKS // 2, MIX_ROWS, in_w), F32),
            pltpu.VMEM((N_CHUNKS // 2, MIX_ROWS, in_w), F32),
            pltpu.VMEM((MIX_ROWS, dl + dc), BF16),
            pltpu.VMEM((MIX_ROWS, dl + dc), BF16),
            pltpu.VMEM((hl_rows + MIX_ROWS, dl), F32),
            pltpu.VMEM((hc_rows + MIX_ROWS, dc), F32),
            pltpu.VMEM((GROUP, dl), F32),
            pltpu.VMEM((n_taps_c * GROUP, dc), F32),
        ],
        compiler_params=_params(1),
        name="mixer",
    )(x, x, mod, mod_out, h0, lbuf, cbuf, *consts)


def _block_diag(w):
    heads, di, dj = w.shape
    eye = jnp.eye(heads, dtype=w.dtype)
    return jnp.einsum("hij,hk->hikj", w, eye).reshape(heads * di, heads * dj)


def _gate_weights(wa, wx, ba, bx):
    heads, hd, _ = wa.shape
    per = max(1, min(heads, V7X_MXU_DIM // hd))
    while heads % per:
        per -= 1
    n_grp, gsz = heads // per, per * hd
    ws, bs = [], []
    for gi in range(n_grp):
        hsl = slice(gi * per, (gi + 1) * per)
        csl = slice(gi * gsz, (gi + 1) * gsz)
        ws.append(jnp.concatenate([_block_diag(wa[hsl]), _block_diag(wx[hsl])], axis=1))
        bs.append(jnp.concatenate([ba[csl], bx[csl]]).reshape(1, -1))
    return jnp.stack(ws).astype(BF16), jnp.stack(bs)


def _to_groups(state):
    bsz, r, c = state.shape
    return state.reshape(bsz // GROUP, GROUP, r, c).transpose(0, 2, 1, 3).reshape(
        bsz // GROUP, r * GROUP, c)


def _from_groups(state, r):
    n_grp, _, c = state.shape
    return state.reshape(n_grp, r, GROUP, c).transpose(0, 2, 1, 3).reshape(n_grp * GROUP, r, c)


def _stream_layer(x, mod, h0, lbuf, cbuf, w, final_g):
    bsz, t, _ = x.shape
    assert bsz % GROUP == 0
    tq = min(t, 128)
    hl, hc = lbuf.shape[1], cbuf.shape[1]
    x = _ffn_in(x, mod, w["ffn1_g"], w["ffn1_wg"], w["ffn1_wu"], w["ffn1_wd"], mod_base=0, tq=tq)
    x, h_last, ltail, ctail = _mixer(
        x, mod, h0.reshape(bsz // GROUP, GROUP, -1), _to_groups(lbuf), _to_groups(cbuf), w)
    x = _ffn_out(x, mod, w["ffn2_g"], w["ffn2_wg"], w["ffn2_wu"], w["ffn2_wd"], final_g,
                 mod_base=6, tq=tq)
    return x, h_last.reshape(bsz, -1), _from_groups(ltail, hl), _from_groups(ctail, hc)


def kernel(x_prompt, x_sample, state_lru_h, state_lru_conv, state_cm_conv, c_prompt, c_sample, w_ada, b_ada, ffn1_g, ffn1_wg, ffn1_wu, ffn1_wd, mix_g, w_in, lru_conv_w, lru_conv_b, lru_wa, lru_ba, lru_wx, lru_bx, lru_lambda, cm_dw_w, cm_dw_b, cm_ln_g, cm_ln_b, out_g_lru, out_g_conv, w_out, ffn2_g, ffn2_wg, ffn2_wu, ffn2_wd, final_g):
    depth = w_ada.shape[0]
    if depth == 0:
        raise ValueError("at least one layer is required")
    bp = x_prompt.shape[0]
    dl = lru_lambda.shape[-1]
    dc = cm_dw_b.shape[-1]
    hl = lru_conv_w.shape[1] - 1
    hc = cm_dw_w.shape[1] - 1
    xp, xs = x_prompt, x_sample
    outs = [[] for _ in range(6)]
    for l in range(depth):
        row = lambda v: v[l].reshape(1, -1)
        w_gate, b_gate = _gate_weights(lru_wa[l], lru_wx[l], lru_ba[l], lru_bx[l])
        w = {
            "ffn1_g": ffn1_g[l], "ffn1_wg": ffn1_wg[l].astype(BF16),
            "ffn1_wu": ffn1_wu[l].astype(BF16), "ffn1_wd": ffn1_wd[l].astype(BF16),
            "ffn2_g": ffn2_g[l], "ffn2_wg": ffn2_wg[l].astype(BF16),
            "ffn2_wu": ffn2_wu[l].astype(BF16), "ffn2_wd": ffn2_wd[l].astype(BF16),
            "mix_g": row(mix_g), "w_in": w_in[l].astype(BF16),
            "lru_conv_w": lru_conv_w[l], "lru_conv_b": row(lru_conv_b),
            "w_gate": w_gate, "b_gate": b_gate,
            "lru_lambda": row(lru_lambda),
            "cm_dw_w": cm_dw_w[l], "cm_dw_b": row(cm_dw_b),
            "cm_ln_g": row(cm_ln_g), "cm_ln_b": row(cm_ln_b),
            "out_g_lru": row(out_g_lru), "out_g_conv": row(out_g_conv),
            "w_out": w_out[l].astype(BF16),
        }
        fg = final_g if l == depth - 1 else None
        mod = _ada(jnp.concatenate([c_prompt, c_sample], axis=0), w_ada[l], b_ada[l])
        xp, hp, lcp, ccp = _stream_layer(
            xp, mod[:, :bp], jnp.zeros((bp, dl), F32), jnp.zeros((bp, hl, dl), F32),
            jnp.zeros((bp, hc, dc), F32), w, fg)
        xs, hs, lcs, ccs = _stream_layer(
            xs, mod[:, bp:], state_lru_h[l], state_lru_conv[l], state_cm_conv[l], w, fg)
        for lst, v in zip(outs, (hp, lcp, ccp, hs, lcs, ccs)):
            lst.append(v)
    return (xp, xs) + tuple(jnp.stack(v, axis=0) for v in outs)
```

```python
import functools

import jax
import jax.numpy as jnp
from jax.experimental import pallas as pl
from jax.experimental.pallas import tpu as pltpu

EPS = 1e-6
LRU_C = 8.0
N_MOD = 9

V7X_SUBLANES = 8
V7X_LANES = 128
V7X_MXU_DIM = 256
V7X_VMEM_LIMIT_BYTES = 56 * 1024 * 1024

GROUP = V7X_SUBLANES
CONV_BLOCK = 8

F32 = jnp.float32
BF16 = jnp.bfloat16


def _dot(a, b):
    return jnp.dot(a, b, preferred_element_type=F32)


def _rms(xf, g):
    return xf * jax.lax.rsqrt(jnp.mean(xf * xf, axis=-1, keepdims=True) + EPS) * g


def _const_spec(shape):
    zeros = (0,) * len(shape)
    return pl.BlockSpec(shape, lambda *_: zeros, pipeline_mode=pl.Buffered(1))


def _params(n_grid):
    return pltpu.CompilerParams(
        dimension_semantics=("arbitrary",) * n_grid, vmem_limit_bytes=V7X_VMEM_LIMIT_BYTES)


def _ada_body(c_ref, w_ref, b_ref, o_ref):
    c = c_ref[...]
    s = (c * jax.nn.sigmoid(c)).astype(BF16)
    o_ref[0] = _dot(s, w_ref[...].astype(BF16)) + b_ref[...]


def _ada(c, w_ada, b_ada):
    bsz, d = c.shape
    n_mod = w_ada.shape[1] // d
    return pl.pallas_call(
        _ada_body,
        grid=(n_mod,),
        in_specs=[
            pl.BlockSpec((bsz, d), lambda j: (0, 0)),
            pl.BlockSpec((d, d), lambda j: (0, j)),
            pl.BlockSpec((1, d), lambda j: (0, j)),
        ],
        out_specs=pl.BlockSpec((1, bsz, d), lambda j: (j, 0, 0)),
        out_shape=jax.ShapeDtypeStruct((n_mod, bsz, d), F32),
        compiler_params=_params(1),
        name="ada_mod",
    )(c, w_ada, b_ada.reshape(1, -1))


def _ffn_chunks(d_ff, width):
    out, lo = [], 0
    while lo < d_ff:
        sz = min(width, d_ff - lo)
        out.append((lo, sz))
        lo += sz
    return tuple(out)


def _swiglu(hb, wg_ref, wu_ref, wd_ref, chunks):
    acc = None
    for lo, sz in chunks:
        g = _dot(hb, wg_ref[:, lo:lo + sz])
        u = _dot(hb, wu_ref[:, lo:lo + sz])
        a = (g * jax.nn.sigmoid(g) * u).astype(BF16)
        p = _dot(a, wd_ref[lo:lo + sz, :])
        acc = p if acc is None else acc + p
    return acc


def _ffn_in_body(x_ref, mod_ref, g_ref, wg_ref, wu_ref, wd_ref, o_ref, *, mod_base, chunks):
    nb, tq, d = x_ref.shape
    hs = []
    for b in range(nb):
        sh = mod_ref[mod_base, b:b + 1, :]
        sc = mod_ref[mod_base + 1, b:b + 1, :]
        hs.append((_rms(x_ref[b], g_ref[...]) * (1.0 + sc) + sh).astype(BF16))
    acc = _swiglu(jnp.concatenate(hs, axis=0), wg_ref, wu_ref, wd_ref, chunks)
    for b in range(nb):
        gt = mod_ref[mod_base + 2, b:b + 1, :]
        y = x_ref[b] + (0.5 * gt) * acc[b * tq:(b + 1) * tq, :]
        for s in range(d // V7X_LANES):
            o_ref[0, s, pl.ds(b, tq, stride=nb), :] = y[:, s * V7X_LANES:(s + 1) * V7X_LANES]


def _ffn_out_body(*refs, mod_base, chunks, final):
    if final:
        x_ref, mod_ref, g_ref, wg_ref, wu_ref, wd_ref, fg_ref, o_ref, y_s = refs
    else:
        x_ref, mod_ref, g_ref, wg_ref, wu_ref, wd_ref, o_ref, y_s = refs
    nb, tq, d = o_ref.shape
    rows = nb * tq
    x = x_ref[0].reshape(tq, nb, d)
    sh = mod_ref[mod_base][None]
    sc = mod_ref[mod_base + 1][None]
    gt = mod_ref[mod_base + 2][None]
    h = _rms(x, g_ref[...]) * (1.0 + sc) + sh
    acc = _swiglu(h.reshape(rows, d).astype(BF16), wg_ref, wu_ref, wd_ref, chunks)
    y = x + (0.5 * gt) * acc.reshape(tq, nb, d)
    if final:
        y = _rms(y, fg_ref[...])
    y = y.reshape(rows, d)
    n_slab = d // V7X_LANES
    for s in range(n_slab):
        y_s[s] = y[:, s * V7X_LANES:(s + 1) * V7X_LANES]
    for b in range(nb):
        o_ref[b] = jnp.concatenate(
            [y_s[s, pl.ds(b, tq, stride=nb), :] for s in range(n_slab)], axis=-1)


def _ffn_in(x, mod, g, wg, wu, wd, *, mod_base, tq):
    bsz, t, d = x.shape
    d_ff = wg.shape[1]
    n_grp, n_slab = bsz // GROUP, d // V7X_LANES
    body = functools.partial(
        _ffn_in_body, mod_base=mod_base, chunks=_ffn_chunks(d_ff, 3 * V7X_MXU_DIM))
    return pl.pallas_call(
        body,
        grid=(n_grp, t // tq),
        in_specs=[
            pl.BlockSpec((GROUP, tq, d), lambda b, i: (b, i, 0)),
            pl.BlockSpec((N_MOD, GROUP, d), lambda b, i: (0, b, 0)),
            _const_spec((1, d)),
            _const_spec((d, d_ff)),
            _const_spec((d, d_ff)),
            _const_spec((d_ff, d)),
        ],
        out_specs=pl.BlockSpec((1, n_slab, tq * GROUP, V7X_LANES), lambda b, i: (b, 0, i, 0)),
        out_shape=jax.ShapeDtypeStruct((n_grp, n_slab, t * GROUP, V7X_LANES), F32),
        compiler_params=_params(2),
        name="ffn_in",
    )(x, mod, g.reshape(1, d), wg, wu, wd)


def _ffn_out(x, mod, g, wg, wu, wd, final_g, *, mod_base, tq):
    n_grp, rows_all, d = x.shape
    t = rows_all // GROUP
    d_ff = wg.shape[1]
    final = final_g is not None
    body = functools.partial(
        _ffn_out_body, mod_base=mod_base, chunks=_ffn_chunks(d_ff, 3 * V7X_MXU_DIM), final=final)
    in_specs = [
        pl.BlockSpec((1, tq * GROUP, d), lambda b, i: (b, i, 0)),
        pl.BlockSpec((N_MOD, GROUP, d), lambda b, i: (0, b, 0)),
        _const_spec((1, d)),
        _const_spec((d, d_ff)),
        _const_spec((d, d_ff)),
        _const_spec((d_ff, d)),
    ]
    args = [x, mod, g.reshape(1, d), wg, wu, wd]
    if final:
        in_specs.append(_const_spec((1, d)))
        args.append(final_g.reshape(1, d))
    return pl.pallas_call(
        body,
        grid=(n_grp, t // tq),
        in_specs=in_specs,
        out_specs=pl.BlockSpec((GROUP, tq, d), lambda b, i: (b, i, 0)),
        out_shape=jax.ShapeDtypeStruct((n_grp * GROUP, t, d), F32),
        scratch_shapes=[pltpu.VMEM((d // V7X_LANES, tq * GROUP, V7X_LANES), F32)],
        compiler_params=_params(2),
        name="ffn_out_final" if final else "ffn_out",
    )(*args)


def _zero_after(tiles):
    bits = None
    for t in tiles:
        w = jax.lax.shift_right_logical(pltpu.bitcast(t, jnp.uint32), jnp.uint32(16))
        w = jax.lax.shift_right_logical(w, jnp.uint32(16))
        bits = w if bits is None else bits | w
    return pltpu.bitcast(bits, F32)


def _dwconv(buf_ref, w_ref, b_ref, rows, n_taps):
    width = buf_ref.shape[-1]
    n_out = rows // GROUP
    block = min(CONV_BLOCK, n_out)
    assert n_out % block == 0
    col_blocks = []
    prev = None
    for c0 in range(0, width, V7X_LANES):
        lanes = slice(c0, c0 + V7X_LANES)
        outs = []
        for j0 in range(0, n_out, block):
            bias = jnp.broadcast_to(b_ref[:, lanes], (GROUP, V7X_LANES))
            if prev is not None:
                bias = bias + _zero_after(prev)
            accs = [bias] * block
            tiles = {}
            for k in range(n_taps):
                tap = jnp.broadcast_to(w_ref[k:k + 1, lanes], (GROUP, V7X_LANES))
                for r in range(block):
                    j = j0 + r + k
                    if j not in tiles:
                        tiles[j] = buf_ref[j * GROUP:(j + 1) * GROUP, lanes]
                    accs[r] = accs[r] + tap * tiles[j]
            prev = accs
            outs.extend(accs)
        col_blocks.append(jnp.concatenate(outs, axis=0))
    return jnp.concatenate(col_blocks, axis=-1)


def _mix_body(x_ref, mod_ref, h0_ref, lbuf0_ref, cbuf0_ref,
              mixg_ref, win_ref, lcw_ref, lcb_ref, wgate_ref, bgate_ref, lam_ref,
              cmw_ref, cmb_ref, lng_ref, lnb_ref, ogl_ref, ogc_ref, wout_ref,
              o_ref, ht_ref, ltail_ref, ctail_ref,
              lx_s, cu_s, h_s, hseq_s):
    t_idx = pl.program_id(1)
    rows, d = o_ref.shape[1:]
    tt = rows // GROUP
    dl = lx_s.shape[-1]
    dc = cu_s.shape[-1]
    hl_rows = lbuf0_ref.shape[1]
    hc_rows = cbuf0_ref.shape[1]
    n_gate_grp, gsz = wgate_ref.shape[0], wgate_ref.shape[1]

    @pl.when(t_idx == 0)
    def _load_state():
        lx_s[:hl_rows, :] = lbuf0_ref[0]
        cu_s[:hc_rows, :] = cbuf0_ref[0]
        h_s[...] = h0_ref[0]

    x = jnp.concatenate([x_ref[0, s] for s in range(x_ref.shape[1])], axis=-1)
    x3 = x.reshape(tt, GROUP, d)
    sh = mod_ref[3][None]
    sc = mod_ref[4][None]
    gt = mod_ref[5][None]
    h = _rms(x3, mixg_ref[...]) * (1.0 + sc) + sh
    z = _dot(h.reshape(rows, d).astype(BF16), win_ref[...])
    z_cv = z[:, :dc]
    z_cg = z[:, dc:2 * dc]
    z_lx = z[:, 2 * dc:2 * dc + dl]
    z_lg = z[:, 2 * dc + dl:]

    lx_s[hl_rows:, :] = z_lx
    y = _dwconv(lx_s, lcw_ref, lcb_ref, rows, hl_rows // GROUP + 1)
    yb = y.astype(BF16)
    r_parts, i_parts = [], []
    for gi in range(n_gate_grp):
        gates = _dot(yb[:, gi * gsz:(gi + 1) * gsz], wgate_ref[gi]) + bgate_ref[gi]
        r_parts.append(jax.nn.sigmoid(gates[:, :gsz]))
        i_parts.append(jax.nn.sigmoid(gates[:, gsz:]))
    r = jnp.concatenate(r_parts, axis=-1)
    i = jnp.concatenate(i_parts, axis=-1)
    log_a = -LRU_C * r * jax.nn.softplus(-lam_ref[...])
    a = jnp.exp(log_a)
    b = jnp.sqrt(-jnp.tanh(log_a) * (a * a + 1.0)) * (i * y)

    carry = h_s[...]
    for j in range(tt):
        rs = slice(j * GROUP, (j + 1) * GROUP)
        carry = a[rs, :] * carry + b[rs, :]
        hseq_s[rs, :] = carry
    h_s[...] = carry
    lru_out = hseq_s[...] * jax.nn.gelu(z_lg)
    n_lru = _rms(lru_out, ogl_ref[...])

    cu_s[hc_rows:, :] = z_cv * jax.nn.sigmoid(z_cg)
    v = _dwconv(cu_s, cmw_ref, cmb_ref, rows, hc_rows // GROUP + 1)
    mu = jnp.mean(v, axis=-1, keepdims=True)
    vc = v - mu
    vn = vc * jax.lax.rsqrt(jnp.mean(vc * vc, axis=-1, keepdims=True) + EPS)
    vn = vn * lng_ref[...] + lnb_ref[...]
    vs = vn * jax.nn.sigmoid(vn)
    n_cv = _rms(vs, ogc_ref[...])

    merged = _dot(jnp.concatenate([n_cv, n_lru], axis=-1).astype(BF16), wout_ref[...])
    o_ref[0] = (x3 + gt * merged.reshape(tt, GROUP, d)).reshape(rows, d)

    lx_s[:hl_rows, :] = lx_s[rows:rows + hl_rows, :]
    cu_s[:hc_rows, :] = cu_s[rows:rows + hc_rows, :]
    ht_ref[0] = h_s[...]
    ltail_ref[0] = lx_s[:hl_rows, :]
    ctail_ref[0] = cu_s[:hc_rows, :]


def _mixer(x, mod, h0, lbuf, cbuf, w, *, tt):
    n_grp, n_slab, rows_all, _ = x.shape
    d = n_slab * V7X_LANES
    t = rows_all // GROUP
    rows = tt * GROUP
    dl, dc = h0.shape[-1], cbuf.shape[-1]
    hl_rows, hc_rows = lbuf.shape[1], cbuf.shape[1]
    assert t % tt == 0 and rows >= hc_rows and rows >= hl_rows

    def grp_spec(r, c):
        return pl.BlockSpec((1, r, c), lambda b, i: (b, 0, 0))

    consts = [w["mix_g"], w["w_in"], w["lru_conv_w"], w["lru_conv_b"], w["w_gate"], w["b_gate"],
              w["lru_lambda"], w["cm_dw_w"], w["cm_dw_b"], w["cm_ln_g"], w["cm_ln_b"],
              w["out_g_lru"], w["out_g_conv"], w["w_out"]]
    return pl.pallas_call(
        _mix_body,
        grid=(n_grp, t // tt),
        in_specs=[
            pl.BlockSpec((1, n_slab, rows, V7X_LANES), lambda b, i: (b, 0, i, 0)),
            pl.BlockSpec((N_MOD, GROUP, d), lambda b, i: (0, b, 0)),
            grp_spec(GROUP, dl), grp_spec(hl_rows, dl), grp_spec(hc_rows, dc),
        ] + [_const_spec(c.shape) for c in consts],
        out_specs=[
            pl.BlockSpec((1, rows, d), lambda b, i: (b, i, 0)),
            grp_spec(GROUP, dl), grp_spec(hl_rows, dl), grp_spec(hc_rows, dc),
        ],
        out_shape=[
            jax.ShapeDtypeStruct((n_grp, rows_all, d), F32),
            jax.ShapeDtypeStruct((n_grp, GROUP, dl), F32),
            jax.ShapeDtypeStruct((n_grp, hl_rows, dl), F32),
            jax.ShapeDtypeStruct((n_grp, hc_rows, dc), F32),
        ],
        scratch_shapes=[
            pltpu.VMEM((hl_rows + rows, dl), F32),
            pltpu.VMEM((hc_rows + rows, dc), F32),
            pltpu.VMEM((GROUP, dl), F32),
            pltpu.VMEM((rows, dl), F32),
        ],
        compiler_params=_params(2),
        name="mixer",
    )(x, mod, h0, lbuf, cbuf, *consts)


def _block_diag(w):
    heads, di, dj = w.shape
    eye = jnp.eye(heads, dtype=w.dtype)
    return jnp.einsum("hij,hk->hikj", w, eye).reshape(heads * di, heads * dj)


def _gate_weights(wa, wx, ba, bx):
    heads, hd, _ = wa.shape
    per = max(1, min(heads, V7X_MXU_DIM // hd))
    while heads % per:
        per -= 1
    n_grp, gsz = heads // per, per * hd
    ws, bs = [], []
    for gi in range(n_grp):
        hsl = slice(gi * per, (gi + 1) * per)
        csl = slice(gi * gsz, (gi + 1) * gsz)
        ws.append(jnp.concatenate([_block_diag(wa[hsl]), _block_diag(wx[hsl])], axis=1))
        bs.append(jnp.concatenate([ba[csl], bx[csl]]).reshape(1, -1))
    return jnp.stack(ws).astype(BF16), jnp.stack(bs)


def _to_groups(state):
    bsz, r, c = state.shape
    return state.reshape(bsz // GROUP, GROUP, r, c).transpose(0, 2, 1, 3).reshape(
        bsz // GROUP, r * GROUP, c)


def _from_groups(state, r):
    n_grp, _, c = state.shape
    return state.reshape(n_grp, r, GROUP, c).transpose(0, 2, 1, 3).reshape(n_grp * GROUP, r, c)


def _stream_layer(x, mod, h0, lbuf, cbuf, w, final_g):
    bsz, t, _ = x.shape
    assert bsz % GROUP == 0
    tq = min(t, 128)
    tt = min(t, 64)
    hl, hc = lbuf.shape[1], cbuf.shape[1]
    x = _ffn_in(x, mod, w["ffn1_g"], w["ffn1_wg"], w["ffn1_wu"], w["ffn1_wd"], mod_base=0, tq=tq)
    x, h_last, ltail, ctail = _mixer(
        x, mod, h0.reshape(bsz // GROUP, GROUP, -1), _to_groups(lbuf), _to_groups(cbuf), w, tt=tt)
    x = _ffn_out(x, mod, w["ffn2_g"], w["ffn2_wg"], w["ffn2_wu"], w["ffn2_wd"], final_g,
                 mod_base=6, tq=tq)
    return x, h_last.reshape(bsz, -1), _from_groups(ltail, hl), _from_groups(ctail, hc)


def kernel(x_prompt, x_sample, state_lru_h, state_lru_conv, state_cm_conv, c_prompt, c_sample, w_ada, b_ada, ffn1_g, ffn1_wg, ffn1_wu, ffn1_wd, mix_g, w_in, lru_conv_w, lru_conv_b, lru_wa, lru_ba, lru_wx, lru_bx, lru_lambda, cm_dw_w, cm_dw_b, cm_ln_g, cm_ln_b, out_g_lru, out_g_conv, w_out, ffn2_g, ffn2_wg, ffn2_wu, ffn2_wd, final_g):
    depth = w_ada.shape[0]
    if depth == 0:
        raise ValueError("at least one layer is required")
    bp = x_prompt.shape[0]
    dl = lru_lambda.shape[-1]
    dc = cm_dw_b.shape[-1]
    hl = lru_conv_w.shape[1] - 1
    hc = cm_dw_w.shape[1] - 1
    xp, xs = x_prompt, x_sample
    outs = [[] for _ in range(6)]
    for l in range(depth):
        row = lambda v: v[l].reshape(1, -1)
        w_gate, b_gate = _gate_weights(lru_wa[l], lru_wx[l], lru_ba[l], lru_bx[l])
        w = {
            "ffn1_g": ffn1_g[l], "ffn1_wg": ffn1_wg[l].astype(BF16),
            "ffn1_wu": ffn1_wu[l].astype(BF16), "ffn1_wd": ffn1_wd[l].astype(BF16),
            "ffn2_g": ffn2_g[l], "ffn2_wg": ffn2_wg[l].astype(BF16),
            "ffn2_wu": ffn2_wu[l].astype(BF16), "ffn2_wd": ffn2_wd[l].astype(BF16),
            "mix_g": row(mix_g),
            "w_in": jnp.concatenate([w_in[l][:, 2 * dl:], w_in[l][:, :2 * dl]], axis=1).astype(BF16),
            "w_out": jnp.concatenate([w_out[l][dl:], w_out[l][:dl]], axis=0).astype(BF16),
            "lru_conv_w": lru_conv_w[l], "lru_conv_b": row(lru_conv_b),
            "w_gate": w_gate, "b_gate": b_gate,
            "lru_lambda": row(lru_lambda),
            "cm_dw_w": cm_dw_w[l], "cm_dw_b": row(cm_dw_b),
            "cm_ln_g": row(cm_ln_g), "cm_ln_b": row(cm_ln_b),
            "out_g_lru": row(out_g_lru), "out_g_conv": row(out_g_conv),
        }
        fg = final_g if l == depth - 1 else None
        mod = _ada(jnp.concatenate([c_prompt, c_sample], axis=0), w_ada[l], b_ada[l])
        xp, hp, lcp, ccp = _stream_layer(
            xp, mod[:, :bp], jnp.zeros((bp, dl), F32), jnp.zeros((bp, hl, dl), F32),
            jnp.zeros((bp, hc, dc), F32), w, fg)
        xs, hs, lcs, ccs = _stream_layer(
            xs, mod[:, bp:], state_lru_h[l], state_lru_conv[l], state_cm_conv[l], w, fg)
        for lst, v in zip(outs, (hp, lcp, ccp, hs, lcs, ccs)):
            lst.append(v)
    return (xp, xs) + tuple(jnp.stack(v, axis=0) for v in outs)
```

```python
import functools

import jax
import jax.numpy as jnp
from jax.experimental import pallas as pl
from jax.experimental.pallas import tpu as pltpu

EPS = 1e-6
LRU_C = 8.0
N_MOD = 9

V7X_SUBLANES = 8
V7X_LANES = 128
V7X_MXU_DIM = 256
V7X_VMEM_LIMIT_BYTES = 56 * 1024 * 1024

GROUP = V7X_SUBLANES
CONV_BLOCK = 8
MIX_PART_ROWS = 256

F32 = jnp.float32
BF16 = jnp.bfloat16


def _dot(a, b):
    return jnp.dot(a, b, preferred_element_type=F32)


def _rms(xf, g):
    return xf * jax.lax.rsqrt(jnp.mean(xf * xf, axis=-1, keepdims=True) + EPS) * g


def _const_spec(shape):
    zeros = (0,) * len(shape)
    return pl.BlockSpec(shape, lambda *_: zeros, pipeline_mode=pl.Buffered(1))


def _params(n_grid):
    return pltpu.CompilerParams(
        dimension_semantics=("arbitrary",) * n_grid, vmem_limit_bytes=V7X_VMEM_LIMIT_BYTES)


def _ada_body(c_ref, w_ref, b_ref, o_ref):
    c = c_ref[...]
    s = (c * jax.nn.sigmoid(c)).astype(BF16)
    o_ref[0] = _dot(s, w_ref[...].astype(BF16)) + b_ref[...]


def _ada(c, w_ada, b_ada):
    bsz, d = c.shape
    n_mod = w_ada.shape[1] // d
    return pl.pallas_call(
        _ada_body,
        grid=(n_mod,),
        in_specs=[
            pl.BlockSpec((bsz, d), lambda j: (0, 0)),
            pl.BlockSpec((d, d), lambda j: (0, j)),
            pl.BlockSpec((1, d), lambda j: (0, j)),
        ],
        out_specs=pl.BlockSpec((1, bsz, d), lambda j: (j, 0, 0)),
        out_shape=jax.ShapeDtypeStruct((n_mod, bsz, d), F32),
        compiler_params=_params(1),
        name="ada_mod",
    )(c, w_ada, b_ada.reshape(1, -1))


def _ffn_chunks(d_ff, width):
    out, lo = [], 0
    while lo < d_ff:
        sz = min(width, d_ff - lo)
        out.append((lo, sz))
        lo += sz
    return tuple(out)


def _swiglu(hb, wg_ref, wu_ref, wd_ref, chunks):
    acc = None
    for lo, sz in chunks:
        g = _dot(hb, wg_ref[:, lo:lo + sz])
        u = _dot(hb, wu_ref[:, lo:lo + sz])
        a = (g * jax.nn.sigmoid(g) * u).astype(BF16)
        p = _dot(a, wd_ref[lo:lo + sz, :])
        acc = p if acc is None else acc + p
    return acc


def _ffn_in_body(x_ref, mod_ref, g_ref, wg_ref, wu_ref, wd_ref, o_ref, *, mod_base, chunks):
    nb, tq, d = x_ref.shape
    hs = []
    for b in range(nb):
        sh = mod_ref[mod_base, b:b + 1, :]
        sc = mod_ref[mod_base + 1, b:b + 1, :]
        hs.append((_rms(x_ref[b], g_ref[...]) * (1.0 + sc) + sh).astype(BF16))
    acc = _swiglu(jnp.concatenate(hs, axis=0), wg_ref, wu_ref, wd_ref, chunks)
    for b in range(nb):
        gt = mod_ref[mod_base + 2, b:b + 1, :]
        y = x_ref[b] + (0.5 * gt) * acc[b * tq:(b + 1) * tq, :]
        for s in range(d // V7X_LANES):
            o_ref[0, s, pl.ds(b, tq, stride=nb), :] = y[:, s * V7X_LANES:(s + 1) * V7X_LANES]


def _ffn_out_body(*refs, mod_base, chunks, final):
    if final:
        x_ref, mod_ref, g_ref, wg_ref, wu_ref, wd_ref, fg_ref, o_ref, y_s = refs
    else:
        x_ref, mod_ref, g_ref, wg_ref, wu_ref, wd_ref, o_ref, y_s = refs
    nb, tq, d = o_ref.shape
    rows = nb * tq
    x = x_ref[0].reshape(tq, nb, d)
    sh = mod_ref[mod_base][None]
    sc = mod_ref[mod_base + 1][None]
    gt = mod_ref[mod_base + 2][None]
    h = _rms(x, g_ref[...]) * (1.0 + sc) + sh
    acc = _swiglu(h.reshape(rows, d).astype(BF16), wg_ref, wu_ref, wd_ref, chunks)
    y = x + (0.5 * gt) * acc.reshape(tq, nb, d)
    if final:
        y = _rms(y, fg_ref[...])
    y = y.reshape(rows, d)
    n_slab = d // V7X_LANES
    for s in range(n_slab):
        y_s[s] = y[:, s * V7X_LANES:(s + 1) * V7X_LANES]
    for b in range(nb):
        o_ref[b] = jnp.concatenate(
            [y_s[s, pl.ds(b, tq, stride=nb), :] for s in range(n_slab)], axis=-1)


def _ffn_in(x, mod, g, wg, wu, wd, *, mod_base, tq):
    bsz, t, d = x.shape
    d_ff = wg.shape[1]
    n_grp, n_slab = bsz // GROUP, d // V7X_LANES
    body = functools.partial(
        _ffn_in_body, mod_base=mod_base, chunks=_ffn_chunks(d_ff, 3 * V7X_MXU_DIM))
    return pl.pallas_call(
        body,
        grid=(n_grp, t // tq),
        in_specs=[
            pl.BlockSpec((GROUP, tq, d), lambda b, i: (b, i, 0)),
            pl.BlockSpec((N_MOD, GROUP, d), lambda b, i: (0, b, 0)),
            _const_spec((1, d)),
            _const_spec((d, d_ff)),
            _const_spec((d, d_ff)),
            _const_spec((d_ff, d)),
        ],
        out_specs=pl.BlockSpec((1, n_slab, tq * GROUP, V7X_LANES), lambda b, i: (b, 0, i, 0)),
        out_shape=jax.ShapeDtypeStruct((n_grp, n_slab, t * GROUP, V7X_LANES), F32),
        compiler_params=_params(2),
        name="ffn_in",
    )(x, mod, g.reshape(1, d), wg, wu, wd)


def _ffn_out(x, mod, g, wg, wu, wd, final_g, *, mod_base, tq):
    n_grp, rows_all, d = x.shape
    t = rows_all // GROUP
    d_ff = wg.shape[1]
    final = final_g is not None
    body = functools.partial(
        _ffn_out_body, mod_base=mod_base, chunks=_ffn_chunks(d_ff, 3 * V7X_MXU_DIM), final=final)
    in_specs = [
        pl.BlockSpec((1, tq * GROUP, d), lambda b, i: (b, i, 0)),
        pl.BlockSpec((N_MOD, GROUP, d), lambda b, i: (0, b, 0)),
        _const_spec((1, d)),
        _const_spec((d, d_ff)),
        _const_spec((d, d_ff)),
        _const_spec((d_ff, d)),
    ]
    args = [x, mod, g.reshape(1, d), wg, wu, wd]
    if final:
        in_specs.append(_const_spec((1, d)))
        args.append(final_g.reshape(1, d))
    return pl.pallas_call(
        body,
        grid=(n_grp, t // tq),
        in_specs=in_specs,
        out_specs=pl.BlockSpec((GROUP, tq, d), lambda b, i: (b, i, 0)),
        out_shape=jax.ShapeDtypeStruct((n_grp * GROUP, t, d), F32),
        scratch_shapes=[pltpu.VMEM((d // V7X_LANES, tq * GROUP, V7X_LANES), F32)],
        compiler_params=_params(2),
        name="ffn_out_final" if final else "ffn_out",
    )(*args)


def _zero_after(tiles):
    bits = None
    for t in tiles:
        w = jax.lax.shift_right_logical(pltpu.bitcast(t, jnp.uint32), jnp.uint32(16))
        w = jax.lax.shift_right_logical(w, jnp.uint32(16))
        bits = w if bits is None else bits | w
    return pltpu.bitcast(bits, F32)


def _dwconv(buf_ref, w_ref, b_ref, row0, rows, n_taps):
    width = buf_ref.shape[-1]
    n_out = rows // GROUP
    first = row0 // GROUP
    block = min(CONV_BLOCK, n_out)
    assert n_out % block == 0
    col_blocks = []
    prev = None
    for c0 in range(0, width, V7X_LANES):
        lanes = slice(c0, c0 + V7X_LANES)
        outs = []
        for j0 in range(0, n_out, block):
            bias = jnp.broadcast_to(b_ref[:, lanes], (GROUP, V7X_LANES))
            if prev is not None:
                bias = bias + _zero_after(prev)
            accs = [bias] * block
            tiles = {}
            for k in range(n_taps):
                tap = jnp.broadcast_to(w_ref[k:k + 1, lanes], (GROUP, V7X_LANES))
                for r in range(block):
                    j = first + j0 + r + k
                    if j not in tiles:
                        tiles[j] = buf_ref[j * GROUP:(j + 1) * GROUP, lanes]
                    accs[r] = accs[r] + tap * tiles[j]
            prev = accs
            outs.extend(accs)
        col_blocks.append(jnp.concatenate(outs, axis=0))
    return jnp.concatenate(col_blocks, axis=-1)


def _mix_body(x_ref, mod_ref, h0_ref, lbuf0_ref, cbuf0_ref,
              mixg_ref, win_ref, lcw_ref, lcb_ref, wgate_ref, bgate_ref, lam_ref,
              cmw_ref, cmb_ref, lng_ref, lnb_ref, ogl_ref, ogc_ref, wout_ref,
              o_ref, ht_ref, ltail_ref, ctail_ref,
              lx_s, cu_s, h_s, hseq_s):
    t_idx = pl.program_id(1)
    rows, d = o_ref.shape[1:]
    tt = rows // GROUP
    dl = lx_s.shape[-1]
    dc = cu_s.shape[-1]
    hl_rows = lbuf0_ref.shape[1]
    hc_rows = cbuf0_ref.shape[1]
    n_gate_grp, gsz = wgate_ref.shape[0], wgate_ref.shape[1]

    @pl.when(t_idx == 0)
    def _load_state():
        lx_s[:hl_rows, :] = lbuf0_ref[0]
        cu_s[:hc_rows, :] = cbuf0_ref[0]
        h_s[...] = h0_ref[0]

    x = jnp.concatenate([x_ref[0, s] for s in range(x_ref.shape[1])], axis=-1)
    x3 = x.reshape(tt, GROUP, d)
    sh = mod_ref[3][None]
    sc = mod_ref[4][None]
    gt = mod_ref[5][None]
    hb = (_rms(x3, mixg_ref[...]) * (1.0 + sc) + sh).reshape(rows, d).astype(BF16)

    pr = MIX_PART_ROWS if rows % MIX_PART_ROWS == 0 else rows
    zs = [_dot(hb[r0:r0 + pr, :], win_ref[...]) for r0 in range(0, rows, pr)]
    carry = h_s[...]
    for p, r0 in enumerate(range(0, rows, pr)):
        z = zs[p]
        z_cv = z[:, :dc]
        z_cg = z[:, dc:2 * dc]
        z_lx = z[:, 2 * dc:2 * dc + dl]
        z_lg = z[:, 2 * dc + dl:]

        lx_s[hl_rows + r0:hl_rows + r0 + pr, :] = z_lx
        y = _dwconv(lx_s, lcw_ref, lcb_ref, r0, pr, hl_rows // GROUP + 1)
        yb = y.astype(BF16)
        r_parts, i_parts = [], []
        for gi in range(n_gate_grp):
            gates = _dot(yb[:, gi * gsz:(gi + 1) * gsz], wgate_ref[gi]) + bgate_ref[gi]
            r_parts.append(jax.nn.sigmoid(gates[:, :gsz]))
            i_parts.append(jax.nn.sigmoid(gates[:, gsz:]))
        r = jnp.concatenate(r_parts, axis=-1)
        i = jnp.concatenate(i_parts, axis=-1)
        log_a = -LRU_C * r * jax.nn.softplus(-lam_ref[...])
        a = jnp.exp(log_a)
        b = jnp.sqrt(-jnp.tanh(log_a) * (a * a + 1.0)) * (i * y)

        for j in range(pr // GROUP):
            rs = slice(j * GROUP, (j + 1) * GROUP)
            carry = a[rs, :] * carry + b[rs, :]
            hseq_s[r0 + j * GROUP:r0 + (j + 1) * GROUP, :] = carry
        lru_out = hseq_s[r0:r0 + pr, :] * jax.nn.gelu(z_lg)
        n_lru = _rms(lru_out, ogl_ref[...])

        cu_s[hc_rows + r0:hc_rows + r0 + pr, :] = z_cv * jax.nn.sigmoid(z_cg)
        v = _dwconv(cu_s, cmw_ref, cmb_ref, r0, pr, hc_rows // GROUP + 1)
        mu = jnp.mean(v, axis=-1, keepdims=True)
        vc = v - mu
        vn = vc * jax.lax.rsqrt(jnp.mean(vc * vc, axis=-1, keepdims=True) + EPS)
        vn = vn * lng_ref[...] + lnb_ref[...]
        vs = vn * jax.nn.sigmoid(vn)
        n_cv = _rms(vs, ogc_ref[...])

        merged = _dot(jnp.concatenate([n_cv, n_lru], axis=-1).astype(BF16), wout_ref[...])
        t0, t1 = r0 // GROUP, (r0 + pr) // GROUP
        o_ref[0, r0:r0 + pr, :] = (
            x3[t0:t1] + gt * merged.reshape(t1 - t0, GROUP, d)).reshape(pr, d)
    h_s[...] = carry

    lx_s[:hl_rows, :] = lx_s[rows:rows + hl_rows, :]
    cu_s[:hc_rows, :] = cu_s[rows:rows + hc_rows, :]
    ht_ref[0] = h_s[...]
    ltail_ref[0] = lx_s[:hl_rows, :]
    ctail_ref[0] = cu_s[:hc_rows, :]


def _mixer(x, mod, h0, lbuf, cbuf, w, *, tt):
    n_grp, n_slab, rows_all, _ = x.shape
    d = n_slab * V7X_LANES
    t = rows_all // GROUP
    rows = tt * GROUP
    dl, dc = h0.shape[-1], cbuf.shape[-1]
    hl_rows, hc_rows = lbuf.shape[1], cbuf.shape[1]
    assert t % tt == 0 and rows >= hc_rows and rows >= hl_rows

    def grp_spec(r, c):
        return pl.BlockSpec((1, r, c), lambda b, i: (b, 0, 0))

    consts = [w["mix_g"], w["w_in"], w["lru_conv_w"], w["lru_conv_b"], w["w_gate"], w["b_gate"],
              w["lru_lambda"], w["cm_dw_w"], w["cm_dw_b"], w["cm_ln_g"], w["cm_ln_b"],
              w["out_g_lru"], w["out_g_conv"], w["w_out"]]
    return pl.pallas_call(
        _mix_body,
        grid=(n_grp, t // tt),
        in_specs=[
            pl.BlockSpec((1, n_slab, rows, V7X_LANES), lambda b, i: (b, 0, i, 0)),
            pl.BlockSpec((N_MOD, GROUP, d), lambda b, i: (0, b, 0)),
            grp_spec(GROUP, dl), grp_spec(hl_rows, dl), grp_spec(hc_rows, dc),
        ] + [_const_spec(c.shape) for c in consts],
        out_specs=[
            pl.BlockSpec((1, rows, d), lambda b, i: (b, i, 0)),
            grp_spec(GROUP, dl), grp_spec(hl_rows, dl), grp_spec(hc_rows, dc),
        ],
        out_shape=[
            jax.ShapeDtypeStruct((n_grp, rows_all, d), F32),
            jax.ShapeDtypeStruct((n_grp, GROUP, dl), F32),
            jax.ShapeDtypeStruct((n_grp, hl_rows, dl), F32),
            jax.ShapeDtypeStruct((n_grp, hc_rows, dc), F32),
        ],
        scratch_shapes=[
            pltpu.VMEM((hl_rows + rows, dl), F32),
            pltpu.VMEM((hc_rows + rows, dc), F32),
            pltpu.VMEM((GROUP, dl), F32),
            pltpu.VMEM((rows, dl), F32),
        ],
        compiler_params=_params(2),
        name="mixer",
    )(x, mod, h0, lbuf, cbuf, *consts)


def _block_diag(w):
    heads, di, dj = w.shape
    eye = jnp.eye(heads, dtype=w.dtype)
    return jnp.einsum("hij,hk->hikj", w, eye).reshape(heads * di, heads * dj)


def _gate_weights(wa, wx, ba, bx):
    heads, hd, _ = wa.shape
    per = max(1, min(heads, V7X_MXU_DIM // hd))
    while heads % per:
        per -= 1
    n_grp, gsz = heads // per, per * hd
    ws, bs = [], []
    for gi in range(n_grp):
        hsl = slice(gi * per, (gi + 1) * per)
        csl = slice(gi * gsz, (gi + 1) * gsz)
        ws.append(jnp.concatenate([_block_diag(wa[hsl]), _block_diag(wx[hsl])], axis=1))
        bs.append(jnp.concatenate([ba[csl], bx[csl]]).reshape(1, -1))
    return jnp.stack(ws).astype(BF16), jnp.stack(bs)


def _to_groups(state):
    bsz, r, c = state.shape
    return state.reshape(bsz // GROUP, GROUP, r, c).transpose(0, 2, 1, 3).reshape(
        bsz // GROUP, r * GROUP, c)


def _from_groups(state, r):
    n_grp, _, c = state.shape
    return state.reshape(n_grp, r, GROUP, c).transpose(0, 2, 1, 3).reshape(n_grp * GROUP, r, c)


def _stream_layer(x, mod, h0, lbuf, cbuf, w, final_g):
    bsz, t, _ = x.shape
    assert bsz % GROUP == 0
    tq = min(t, 128)
    tt = min(t, 128)
    hl, hc = lbuf.shape[1], cbuf.shape[1]
    x = _ffn_in(x, mod, w["ffn1_g"], w["ffn1_wg"], w["ffn1_wu"], w["ffn1_wd"], mod_base=0, tq=tq)
    x, h_last, ltail, ctail = _mixer(
        x, mod, h0.reshape(bsz // GROUP, GROUP, -1), _to_groups(lbuf), _to_groups(cbuf), w, tt=tt)
    x = _ffn_out(x, mod, w["ffn2_g"], w["ffn2_wg"], w["ffn2_wu"], w["ffn2_wd"], final_g,
                 mod_base=6, tq=tq)
    return x, h_last.reshape(bsz, -1), _from_groups(ltail, hl), _from_groups(ctail, hc)


def kernel(x_prompt, x_sample, state_lru_h, state_lru_conv, state_cm_conv, c_prompt, c_sample, w_ada, b_ada, ffn1_g, ffn1_wg, ffn1_wu, ffn1_wd, mix_g, w_in, lru_conv_w, lru_conv_b, lru_wa, lru_ba, lru_wx, lru_bx, lru_lambda, cm_dw_w, cm_dw_b, cm_ln_g, cm_ln_b, out_g_lru, out_g_conv, w_out, ffn2_g, ffn2_wg, ffn2_wu, ffn2_wd, final_g):
    depth = w_ada.shape[0]
    if depth == 0:
        raise ValueError("at least one layer is required")
    bp = x_prompt.shape[0]
    dl = lru_lambda.shape[-1]
    dc = cm_dw_b.shape[-1]
    hl = lru_conv_w.shape[1] - 1
    hc = cm_dw_w.shape[1] - 1
    xp, xs = x_prompt, x_sample
    outs = [[] for _ in range(6)]
    for l in range(depth):
        row = lambda v: v[l].reshape(1, -1)
        w_gate, b_gate = _gate_weights(lru_wa[l], lru_wx[l], lru_ba[l], lru_bx[l])
        w = {
            "ffn1_g": ffn1_g[l], "ffn1_wg": ffn1_wg[l].astype(BF16),
            "ffn1_wu": ffn1_wu[l].astype(BF16), "ffn1_wd": ffn1_wd[l].astype(BF16),
            "ffn2_g": ffn2_g[l], "ffn2_wg": ffn2_wg[l].astype(BF16),
            "ffn2_wu": ffn2_wu[l].astype(BF16), "ffn2_wd": ffn2_wd[l].astype(BF16),
            "mix_g": row(mix_g),
            "w_in": jnp.concatenate([w_in[l][:, 2 * dl:], w_in[l][:, :2 * dl]], axis=1).astype(BF16),
            "w_out": jnp.concatenate([w_out[l][dl:], w_out[l][:dl]], axis=0).astype(BF16),
            "lru_conv_w": lru_conv_w[l], "lru_conv_b": row(lru_conv_b),
            "w_gate": w_gate, "b_gate": b_gate,
            "lru_lambda": row(lru_lambda),
            "cm_dw_w": cm_dw_w[l], "cm_dw_b": row(cm_dw_b),
            "cm_ln_g": row(cm_ln_g), "cm_ln_b": row(cm_ln_b),
            "out_g_lru": row(out_g_lru), "out_g_conv": row(out_g_conv),
        }
        fg = final_g if l == depth - 1 else None
        mod = _ada(jnp.concatenate([c_prompt, c_sample], axis=0), w_ada[l], b_ada[l])
        xp, hp, lcp, ccp = _stream_layer(
            xp, mod[:, :bp], jnp.zeros((bp, dl), F32), jnp.zeros((bp, hl, dl), F32),
            jnp.zeros((bp, hc, dc), F32), w, fg)
        xs, hs, lcs, ccs = _stream_layer(
            xs, mod[:, bp:], state_lru_h[l], state_lru_conv[l], state_cm_conv[l], w, fg)
        for lst, v in zip(outs, (hp, lcp, ccp, hs, lcs, ccs)):
            lst.append(v)
    return (xp, xs) + tuple(jnp.stack(v, axis=0) for v in outs)
```

```python
import functools

import jax
import jax.numpy as jnp
from jax.experimental import pallas as pl
from jax.experimental.pallas import tpu as pltpu

EPS = 1e-6
LRU_C = 8.0
N_MOD = 9

V7X_SUBLANES = 8
V7X_LANES = 128
V7X_MXU_DIM = 256
V7X_VMEM_LIMIT_BYTES = 56 * 1024 * 1024

GROUP = V7X_SUBLANES
CONV_BLOCK = 8
MIX_PART_ROWS = 256

F32 = jnp.float32
BF16 = jnp.bfloat16


def _dot(a, b):
    return jnp.dot(a, b, preferred_element_type=F32)


def _rms(xf, g):
    return xf * jax.lax.rsqrt(jnp.mean(xf * xf, axis=-1, keepdims=True) + EPS) * g


def _const_spec(shape):
    zeros = (0,) * len(shape)
    return pl.BlockSpec(shape, lambda *_: zeros, pipeline_mode=pl.Buffered(1))


def _params(n_grid):
    return pltpu.CompilerParams(
        dimension_semantics=("arbitrary",) * n_grid, vmem_limit_bytes=V7X_VMEM_LIMIT_BYTES)


def _ada_body(c_ref, w_ref, b_ref, o_ref):
    c = c_ref[...]
    s = (c * jax.nn.sigmoid(c)).astype(BF16)
    o_ref[0] = _dot(s, w_ref[...].astype(BF16)) + b_ref[...]


def _ada(c, w_ada, b_ada):
    bsz, d = c.shape
    n_mod = w_ada.shape[1] // d
    return pl.pallas_call(
        _ada_body,
        grid=(n_mod,),
        in_specs=[
            pl.BlockSpec((bsz, d), lambda j: (0, 0)),
            pl.BlockSpec((d, d), lambda j: (0, j)),
            pl.BlockSpec((1, d), lambda j: (0, j)),
        ],
        out_specs=pl.BlockSpec((1, bsz, d), lambda j: (j, 0, 0)),
        out_shape=jax.ShapeDtypeStruct((n_mod, bsz, d), F32),
        compiler_params=_params(1),
        name="ada_mod",
    )(c, w_ada, b_ada.reshape(1, -1))


def _ffn_chunks(d_ff, width):
    out, lo = [], 0
    while lo < d_ff:
        sz = min(width, d_ff - lo)
        out.append((lo, sz))
        lo += sz
    return tuple(out)


FFN_PART_ROWS = 256


def _zero_after(tiles):
    bits = None
    for t in tiles:
        w = jax.lax.shift_right_logical(pltpu.bitcast(t, jnp.uint32), jnp.uint32(16))
        w = jax.lax.shift_right_logical(w, jnp.uint32(16))
        bits = w if bits is None else bits | w
    return pltpu.bitcast(bits, F32)


def _after(x, tokens):
    zero = _zero_after(tokens)
    reps = x.shape[0] // GROUP
    head = x[:, :V7X_LANES] + jnp.concatenate([zero] * reps, axis=0)
    return jnp.concatenate([head, x[:, V7X_LANES:]], axis=-1)


def _swiglu_part(hb, next_tokens, wg_ref, wu_ref, wd_ref, chunks):
    acc = None
    for ci, (lo, sz) in enumerate(chunks):
        g = _dot(hb, wg_ref[:, lo:lo + sz])
        u = _dot(hb, wu_ref[:, lo:lo + sz])
        act = g * jax.nn.sigmoid(g) * u
        if ci == 0 and next_tokens:
            act = _after(act, next_tokens)
        p = _dot(act.astype(BF16), wd_ref[lo:lo + sz, :])
        acc = p if acc is None else acc + p
    return acc


def _ffn_in_body(x_ref, mod_ref, g_ref, wg_ref, wu_ref, wd_ref, o_ref, *, mod_base, chunks):
    nb, tq, d = x_ref.shape
    ps = min(nb, max(1, FFN_PART_ROWS // tq))
    assert nb % ps == 0

    def norm(b0):
        hs, toks = [], []
        for b in range(b0, b0 + ps):
            sh = mod_ref[mod_base, b:b + 1, :]
            sc = mod_ref[mod_base + 1, b:b + 1, :]
            h = _rms(x_ref[b], g_ref[...]) * (1.0 + sc) + sh
            toks.extend(h[j * GROUP:(j + 1) * GROUP, :V7X_LANES] for j in range(tq // GROUP))
            hs.append(h.astype(BF16))
        return jnp.concatenate(hs, axis=0), toks

    cur = norm(0)
    for b0 in range(0, nb, ps):
        nxt = norm(b0 + ps) if b0 + ps < nb else (None, [])
        acc = _swiglu_part(cur[0], nxt[1], wg_ref, wu_ref, wd_ref, chunks)
        for b in range(b0, b0 + ps):
            gt = mod_ref[mod_base + 2, b:b + 1, :]
            y = x_ref[b] + (0.5 * gt) * acc[(b - b0) * tq:(b - b0 + 1) * tq, :]
            for s in range(d // V7X_LANES):
                o_ref[0, s, pl.ds(b, tq, stride=nb), :] = y[:, s * V7X_LANES:(s + 1) * V7X_LANES]
        cur = nxt


def _ffn_out_body(*refs, mod_base, chunks, final):
    if final:
        x_ref, mod_ref, g_ref, wg_ref, wu_ref, wd_ref, fg_ref, o_ref, y_s = refs
    else:
        x_ref, mod_ref, g_ref, wg_ref, wu_ref, wd_ref, o_ref, y_s = refs
    nb, tq, d = o_ref.shape
    rows = nb * tq
    n_slab = d // V7X_LANES
    pr = min(rows, max(nb, FFN_PART_ROWS))
    assert rows % pr == 0 and pr % nb == 0
    tp = pr // nb
    sh = mod_ref[mod_base][None]
    sc = mod_ref[mod_base + 1][None]
    gt = mod_ref[mod_base + 2][None]

    def norm(r0):
        x = x_ref[0, r0:r0 + pr, :].reshape(tp, nb, d)
        h = (_rms(x, g_ref[...]) * (1.0 + sc) + sh).reshape(pr, d)
        toks = [h[j * GROUP:(j + 1) * GROUP, :V7X_LANES] for j in range(pr // GROUP)]
        return x, h.astype(BF16), toks

    cur = norm(0)
    for r0 in range(0, rows, pr):
        nxt = norm(r0 + pr) if r0 + pr < rows else (None, None, [])
        acc = _swiglu_part(cur[1], nxt[2], wg_ref, wu_ref, wd_ref, chunks)
        y = cur[0] + (0.5 * gt) * acc.reshape(tp, nb, d)
        if final:
            y = _rms(y, fg_ref[...])
        y = y.reshape(pr, d)
        for s in range(n_slab):
            y_s[s, r0:r0 + pr, :] = y[:, s * V7X_LANES:(s + 1) * V7X_LANES]
        t0 = r0 // nb
        for b in range(nb):
            o_ref[b, t0:t0 + tp, :] = jnp.concatenate(
                [y_s[s, pl.ds(r0 + b, tp, stride=nb), :] for s in range(n_slab)], axis=-1)
        cur = nxt


def _ffn_in(x, mod, g, wg, wu, wd, *, mod_base, tq):
    bsz, t, d = x.shape
    d_ff = wg.shape[1]
    n_grp, n_slab = bsz // GROUP, d // V7X_LANES
    body = functools.partial(
        _ffn_in_body, mod_base=mod_base, chunks=_ffn_chunks(d_ff, 4 * V7X_MXU_DIM))
    return pl.pallas_call(
        body,
        grid=(n_grp, t // tq),
        in_specs=[
            pl.BlockSpec((GROUP, tq, d), lambda b, i: (b, i, 0)),
            pl.BlockSpec((N_MOD, GROUP, d), lambda b, i: (0, b, 0)),
            _const_spec((1, d)),
            _const_spec((d, d_ff)),
            _const_spec((d, d_ff)),
            _const_spec((d_ff, d)),
        ],
        out_specs=pl.BlockSpec((1, n_slab, tq * GROUP, V7X_LANES), lambda b, i: (b, 0, i, 0)),
        out_shape=jax.ShapeDtypeStruct((n_grp, n_slab, t * GROUP, V7X_LANES), F32),
        compiler_params=_params(2),
        name="ffn_in",
    )(x, mod, g.reshape(1, d), wg, wu, wd)


def _ffn_out(x, mod, g, wg, wu, wd, final_g, *, mod_base, tq):
    n_grp, rows_all, d = x.shape
    t = rows_all // GROUP
    d_ff = wg.shape[1]
    final = final_g is not None
    body = functools.partial(
        _ffn_out_body, mod_base=mod_base, chunks=_ffn_chunks(d_ff, 4 * V7X_MXU_DIM), final=final)
    in_specs = [
        pl.BlockSpec((1, tq * GROUP, d), lambda b, i: (b, i, 0)),
        pl.BlockSpec((N_MOD, GROUP, d), lambda b, i: (0, b, 0)),
        _const_spec((1, d)),
        _const_spec((d, d_ff)),
        _const_spec((d, d_ff)),
        _const_spec((d_ff, d)),
    ]
    args = [x, mod, g.reshape(1, d), wg, wu, wd]
    if final:
        in_specs.append(_const_spec((1, d)))
        args.append(final_g.reshape(1, d))
    return pl.pallas_call(
        body,
        grid=(n_grp, t // tq),
        in_specs=in_specs,
        out_specs=pl.BlockSpec((GROUP, tq, d), lambda b, i: (b, i, 0)),
        out_shape=jax.ShapeDtypeStruct((n_grp * GROUP, t, d), F32),
        scratch_shapes=[pltpu.VMEM((d // V7X_LANES, tq * GROUP, V7X_LANES), F32)],
        compiler_params=_params(2),
        name="ffn_out_final" if final else "ffn_out",
    )(*args)


def _dwconv(buf_ref, w_ref, b_ref, row0, rows, n_taps):
    width = buf_ref.shape[-1]
    n_out = rows // GROUP
    first = row0 // GROUP
    block = min(CONV_BLOCK, n_out)
    assert n_out % block == 0
    col_blocks = []
    prev = None
    for c0 in range(0, width, V7X_LANES):
        lanes = slice(c0, c0 + V7X_LANES)
        outs = []
        for j0 in range(0, n_out, block):
            bias = jnp.broadcast_to(b_ref[:, lanes], (GROUP, V7X_LANES))
            if prev is not None:
                bias = bias + _zero_after(prev)
            accs = [bias] * block
            tiles = {}
            for k in range(n_taps):
                tap = jnp.broadcast_to(w_ref[k:k + 1, lanes], (GROUP, V7X_LANES))
                for r in range(block):
                    j = first + j0 + r + k
                    if j not in tiles:
                        tiles[j] = buf_ref[j * GROUP:(j + 1) * GROUP, lanes]
                    accs[r] = accs[r] + tap * tiles[j]
            prev = accs
            outs.extend(accs)
        col_blocks.append(jnp.concatenate(outs, axis=0))
    return jnp.concatenate(col_blocks, axis=-1)


def _mix_body(x_ref, mod_ref, h0_ref, lbuf0_ref, cbuf0_ref,
              mixg_ref, win_ref, lcw_ref, lcb_ref, wgate_ref, bgate_ref, lam_ref,
              cmw_ref, cmb_ref, lng_ref, lnb_ref, ogl_ref, ogc_ref, wout_ref,
              o_ref, ht_ref, ltail_ref, ctail_ref,
              lx_s, cu_s, h_s, hseq_s):
    t_idx = pl.program_id(1)
    rows, d = o_ref.shape[1:]
    tt = rows // GROUP
    dl = lx_s.shape[-1]
    dc = cu_s.shape[-1]
    hl_rows = lbuf0_ref.shape[1]
    hc_rows = cbuf0_ref.shape[1]
    n_gate_grp, gsz = wgate_ref.shape[0], wgate_ref.shape[1]

    @pl.when(t_idx == 0)
    def _load_state():
        lx_s[:hl_rows, :] = lbuf0_ref[0]
        cu_s[:hc_rows, :] = cbuf0_ref[0]
        h_s[...] = h0_ref[0]

    x = jnp.concatenate([x_ref[0, s] for s in range(x_ref.shape[1])], axis=-1)
    x3 = x.reshape(tt, GROUP, d)
    sh = mod_ref[3][None]
    sc = mod_ref[4][None]
    gt = mod_ref[5][None]
    hb = (_rms(x3, mixg_ref[...]) * (1.0 + sc) + sh).reshape(rows, d).astype(BF16)

    pr = MIX_PART_ROWS if rows % MIX_PART_ROWS == 0 else rows
    zs = [_dot(hb[r0:r0 + pr, :], win_ref[...]) for r0 in range(0, rows, pr)]
    carry = h_s[...]
    for p, r0 in enumerate(range(0, rows, pr)):
        z = zs[p]
        z_cv = z[:, :dc]
        z_cg = z[:, dc:2 * dc]
        z_lx = z[:, 2 * dc:2 * dc + dl]
        z_lg = z[:, 2 * dc + dl:]

        lx_s[hl_rows + r0:hl_rows + r0 + pr, :] = z_lx
        y = _dwconv(lx_s, lcw_ref, lcb_ref, r0, pr, hl_rows // GROUP + 1)
        yb = y.astype(BF16)
        r_parts, i_parts = [], []
        for gi in range(n_gate_grp):
            gates = _dot(yb[:, gi * gsz:(gi + 1) * gsz], wgate_ref[gi]) + bgate_ref[gi]
            r_parts.append(jax.nn.sigmoid(gates[:, :gsz]))
            i_parts.append(jax.nn.sigmoid(gates[:, gsz:]))
        r = jnp.concatenate(r_parts, axis=-1)
        i = jnp.concatenate(i_parts, axis=-1)
        log_a = -LRU_C * r * jax.nn.softplus(-lam_ref[...])
        a = jnp.exp(log_a)
        b = jnp.sqrt(-jnp.tanh(log_a) * (a * a + 1.0)) * (i * y)

        for j in range(pr // GROUP):
            rs = slice(j * GROUP, (j + 1) * GROUP)
            carry = a[rs, :] * carry + b[rs, :]
            hseq_s[r0 + j * GROUP:r0 + (j + 1) * GROUP, :] = carry
        lru_out = hseq_s[r0:r0 + pr, :] * jax.nn.gelu(z_lg)
        n_lru = _rms(lru_out, ogl_ref[...])

        cu_s[hc_rows + r0:hc_rows + r0 + pr, :] = z_cv * jax.nn.sigmoid(z_cg)
        v = _dwconv(cu_s, cmw_ref, cmb_ref, r0, pr, hc_rows // GROUP + 1)
        mu = jnp.mean(v, axis=-1, keepdims=True)
        vc = v - mu
        vn = vc * jax.lax.rsqrt(jnp.mean(vc * vc, axis=-1, keepdims=True) + EPS)
        vn = vn * lng_ref[...] + lnb_ref[...]
        vs = vn * jax.nn.sigmoid(vn)
        n_cv = _rms(vs, ogc_ref[...])

        merged = _dot(jnp.concatenate([n_cv, n_lru], axis=-1).astype(BF16), wout_ref[...])
        t0, t1 = r0 // GROUP, (r0 + pr) // GROUP
        o_ref[0, r0:r0 + pr, :] = (
            x3[t0:t1] + gt * merged.reshape(t1 - t0, GROUP, d)).reshape(pr, d)
    h_s[...] = carry

    lx_s[:hl_rows, :] = lx_s[rows:rows + hl_rows, :]
    cu_s[:hc_rows, :] = cu_s[rows:rows + hc_rows, :]
    ht_ref[0] = h_s[...]
    ltail_ref[0] = lx_s[:hl_rows, :]
    ctail_ref[0] = cu_s[:hc_rows, :]


def _mixer(x, mod, h0, lbuf, cbuf, w, *, tt):
    n_grp, n_slab, rows_all, _ = x.shape
    d = n_slab * V7X_LANES
    t = rows_all // GROUP
    rows = tt * GROUP
    dl, dc = h0.shape[-1], cbuf.shape[-1]
    hl_rows, hc_rows = lbuf.shape[1], cbuf.shape[1]
    assert t % tt == 0 and rows >= hc_rows and rows >= hl_rows

    def grp_spec(r, c):
        return pl.BlockSpec((1, r, c), lambda b, i: (b, 0, 0))

    consts = [w["mix_g"], w["w_in"], w["lru_conv_w"], w["lru_conv_b"], w["w_gate"], w["b_gate"],
              w["lru_lambda"], w["cm_dw_w"], w["cm_dw_b"], w["cm_ln_g"], w["cm_ln_b"],
              w["out_g_lru"], w["out_g_conv"], w["w_out"]]
    return pl.pallas_call(
        _mix_body,
        grid=(n_grp, t // tt),
        in_specs=[
            pl.BlockSpec((1, n_slab, rows, V7X_LANES), lambda b, i: (b, 0, i, 0)),
            pl.BlockSpec((N_MOD, GROUP, d), lambda b, i: (0, b, 0)),
            grp_spec(GROUP, dl), grp_spec(hl_rows, dl), grp_spec(hc_rows, dc),
        ] + [_const_spec(c.shape) for c in consts],
        out_specs=[
            pl.BlockSpec((1, rows, d), lambda b, i: (b, i, 0)),
            grp_spec(GROUP, dl), grp_spec(hl_rows, dl), grp_spec(hc_rows, dc),
        ],
        out_shape=[
            jax.ShapeDtypeStruct((n_grp, rows_all, d), F32),
            jax.ShapeDtypeStruct((n_grp, GROUP, dl), F32),
            jax.ShapeDtypeStruct((n_grp, hl_rows, dl), F32),
            jax.ShapeDtypeStruct((n_grp, hc_rows, dc), F32),
        ],
        scratch_shapes=[
            pltpu.VMEM((hl_rows + rows, dl), F32),
            pltpu.VMEM((hc_rows + rows, dc), F32),
            pltpu.VMEM((GROUP, dl), F32),
            pltpu.VMEM((rows, dl), F32),
        ],
        compiler_params=_params(2),
        name="mixer",
    )(x, mod, h0, lbuf, cbuf, *consts)


def _block_diag(w):
    heads, di, dj = w.shape
    eye = jnp.eye(heads, dtype=w.dtype)
    return jnp.einsum("hij,hk->hikj", w, eye).reshape(heads * di, heads * dj)


def _gate_weights(wa, wx, ba, bx):
    heads, hd, _ = wa.shape
    per = max(1, min(heads, V7X_MXU_DIM // hd))
    while heads % per:
        per -= 1
    n_grp, gsz = heads // per, per * hd
    ws, bs = [], []
    for gi in range(n_grp):
        hsl = slice(gi * per, (gi + 1) * per)
        csl = slice(gi * gsz, (gi + 1) * gsz)
        ws.append(jnp.concatenate([_block_diag(wa[hsl]), _block_diag(wx[hsl])], axis=1))
        bs.append(jnp.concatenate([ba[csl], bx[csl]]).reshape(1, -1))
    return jnp.stack(ws).astype(BF16), jnp.stack(bs)


def _to_groups(state):
    bsz, r, c = state.shape
    return state.reshape(bsz // GROUP, GROUP, r, c).transpose(0, 2, 1, 3).reshape(
        bsz // GROUP, r * GROUP, c)


def _from_groups(state, r):
    n_grp, _, c = state.shape
    return state.reshape(n_grp, r, GROUP, c).transpose(0, 2, 1, 3).reshape(n_grp * GROUP, r, c)


def _stream_layer(x, mod, h0, lbuf, cbuf, w, final_g):
    bsz, t, _ = x.shape
    assert bsz % GROUP == 0
    tq = min(t, 128)
    tt = min(t, 128)
    hl, hc = lbuf.shape[1], cbuf.shape[1]
    x = _ffn_in(x, mod, w["ffn1_g"], w["ffn1_wg"], w["ffn1_wu"], w["ffn1_wd"], mod_base=0, tq=tq)
    x, h_last, ltail, ctail = _mixer(
        x, mod, h0.reshape(bsz // GROUP, GROUP, -1), _to_groups(lbuf), _to_groups(cbuf), w, tt=tt)
    x = _ffn_out(x, mod, w["ffn2_g"], w["ffn2_wg"], w["ffn2_wu"], w["ffn2_wd"], final_g,
                 mod_base=6, tq=tq)
    return x, h_last.reshape(bsz, -1), _from_groups(ltail, hl), _from_groups(ctail, hc)


def kernel(x_prompt, x_sample, state_lru_h, state_lru_conv, state_cm_conv, c_prompt, c_sample, w_ada, b_ada, ffn1_g, ffn1_wg, ffn1_wu, ffn1_wd, mix_g, w_in, lru_conv_w, lru_conv_b, lru_wa, lru_ba, lru_wx, lru_bx, lru_lambda, cm_dw_w, cm_dw_b, cm_ln_g, cm_ln_b, out_g_lru, out_g_conv, w_out, ffn2_g, ffn2_wg, ffn2_wu, ffn2_wd, final_g):
    depth = w_ada.shape[0]
    if depth == 0:
        raise ValueError("at least one layer is required")
    bp = x_prompt.shape[0]
    dl = lru_lambda.shape[-1]
    dc = cm_dw_b.shape[-1]
    hl = lru_conv_w.shape[1] - 1
    hc = cm_dw_w.shape[1] - 1
    xp, xs = x_prompt, x_sample
    outs = [[] for _ in range(6)]
    for l in range(depth):
        row = lambda v: v[l].reshape(1, -1)
        w_gate, b_gate = _gate_weights(lru_wa[l], lru_wx[l], lru_ba[l], lru_bx[l])
        w = {
            "ffn1_g": ffn1_g[l], "ffn1_wg": ffn1_wg[l].astype(BF16),
            "ffn1_wu": ffn1_wu[l].astype(BF16), "ffn1_wd": ffn1_wd[l].astype(BF16),
            "ffn2_g": ffn2_g[l], "ffn2_wg": ffn2_wg[l].astype(BF16),
            "ffn2_wu": ffn2_wu[l].astype(BF16), "ffn2_wd": ffn2_wd[l].astype(BF16),
            "mix_g": row(mix_g),
            "w_in": jnp.concatenate([w_in[l][:, 2 * dl:], w_in[l][:, :2 * dl]], axis=1).astype(BF16),
            "w_out": jnp.concatenate([w_out[l][dl:], w_out[l][:dl]], axis=0).astype(BF16),
            "lru_conv_w": lru_conv_w[l], "lru_conv_b": row(lru_conv_b),
            "w_gate": w_gate, "b_gate": b_gate,
            "lru_lambda": row(lru_lambda),
            "cm_dw_w": cm_dw_w[l], "cm_dw_b": row(cm_dw_b),
            "cm_ln_g": row(cm_ln_g), "cm_ln_b": row(cm_ln_b),
            "out_g_lru": row(out_g_lru), "out_g_conv": row(out_g_conv),
        }
        fg = final_g if l == depth - 1 else None
        mod = _ada(jnp.concatenate([c_prompt, c_sample], axis=0), w_ada[l], b_ada[l])
        xp, hp, lcp, ccp = _stream_layer(
            xp, mod[:, :bp], jnp.zeros((bp, dl), F32), jnp.zeros((bp, hl, dl), F32),
            jnp.zeros((bp, hc, dc), F32), w, fg)
        xs, hs, lcs, ccs = _stream_layer(
            xs, mod[:, bp:], state_lru_h[l], state_lru_conv[l], state_cm_conv[l], w, fg)
        for lst, v in zip(outs, (hp, lcp, ccp, hs, lcs, ccs)):
            lst.append(v)
    return (xp, xs) + tuple(jnp.stack(v, axis=0) for v in outs)
```

```python
import functools

import jax
import jax.numpy as jnp
from jax.experimental import pallas as pl
from jax.experimental.pallas import tpu as pltpu

EPS = 1e-6
LRU_C = 8.0
N_MOD = 9

V7X_SUBLANES = 8
V7X_LANES = 128
V7X_MXU_DIM = 256
V7X_VMEM_LIMIT_BYTES = 56 * 1024 * 1024

GROUP = V7X_SUBLANES
CONV_BLOCK = 8
MIX_PART_ROWS = 256

F32 = jnp.float32
BF16 = jnp.bfloat16


def _dot(a, b):
    return jnp.dot(a, b, preferred_element_type=F32)


def _rms(xf, g):
    return xf * jax.lax.rsqrt(jnp.mean(xf * xf, axis=-1, keepdims=True) + EPS) * g


def _const_spec(shape):
    zeros = (0,) * len(shape)
    return pl.BlockSpec(shape, lambda *_: zeros, pipeline_mode=pl.Buffered(1))


def _params(n_grid):
    return pltpu.CompilerParams(
        dimension_semantics=("arbitrary",) * n_grid, vmem_limit_bytes=V7X_VMEM_LIMIT_BYTES)


def _ada_body(c_ref, w_ref, b_ref, o_ref):
    c = c_ref[...]
    s = (c * jax.nn.sigmoid(c)).astype(BF16)
    o_ref[0] = _dot(s, w_ref[...].astype(BF16)) + b_ref[...]


def _ada(c, w_ada, b_ada):
    bsz, d = c.shape
    n_mod = w_ada.shape[1] // d
    return pl.pallas_call(
        _ada_body,
        grid=(n_mod,),
        in_specs=[
            pl.BlockSpec((bsz, d), lambda j: (0, 0)),
            pl.BlockSpec((d, d), lambda j: (0, j)),
            pl.BlockSpec((1, d), lambda j: (0, j)),
        ],
        out_specs=pl.BlockSpec((1, bsz, d), lambda j: (j, 0, 0)),
        out_shape=jax.ShapeDtypeStruct((n_mod, bsz, d), F32),
        compiler_params=_params(1),
        name="ada_mod",
    )(c, w_ada, b_ada.reshape(1, -1))


def _ffn_chunks(d_ff, width):
    out, lo = [], 0
    while lo < d_ff:
        sz = min(width, d_ff - lo)
        out.append((lo, sz))
        lo += sz
    return tuple(out)


FFN_PART_ROWS = 256


def _zero_after(tiles):
    bits = None
    for t in tiles:
        w = jax.lax.shift_right_logical(pltpu.bitcast(t, jnp.uint32), jnp.uint32(16))
        w = jax.lax.shift_right_logical(w, jnp.uint32(16))
        bits = w if bits is None else bits | w
    return pltpu.bitcast(bits, F32)


def _after(x, tokens):
    zero = _zero_after(tokens)
    reps = x.shape[0] // GROUP
    head = x[:, :V7X_LANES] + jnp.concatenate([zero] * reps, axis=0)
    return jnp.concatenate([head, x[:, V7X_LANES:]], axis=-1)


def _swiglu_part(hb, next_tokens, wg_ref, wu_ref, wd_ref, chunks):
    acc = None
    for ci, (lo, sz) in enumerate(chunks):
        g = _dot(hb, wg_ref[:, lo:lo + sz])
        u = _dot(hb, wu_ref[:, lo:lo + sz])
        act = g * jax.nn.sigmoid(g) * u
        if ci == 0 and next_tokens:
            act = _after(act, next_tokens)
        p = _dot(act.astype(BF16), wd_ref[lo:lo + sz, :])
        acc = p if acc is None else acc + p
    return acc


def _ffn_in_body(x_ref, mod_ref, g_ref, wg_ref, wu_ref, wd_ref, o_ref, *, mod_base, chunks):
    nb, tq, d = x_ref.shape
    ps = min(nb, max(1, FFN_PART_ROWS // tq))
    assert nb % ps == 0

    def norm(b0):
        hs, toks = [], []
        for b in range(b0, b0 + ps):
            sh = mod_ref[mod_base, b:b + 1, :]
            sc = mod_ref[mod_base + 1, b:b + 1, :]
            h = _rms(x_ref[b], g_ref[...]) * (1.0 + sc) + sh
            toks.extend(h[j * GROUP:(j + 1) * GROUP, :V7X_LANES] for j in range(tq // GROUP))
            hs.append(h.astype(BF16))
        return jnp.concatenate(hs, axis=0), toks

    cur = norm(0)
    for b0 in range(0, nb, ps):
        nxt = norm(b0 + ps) if b0 + ps < nb else (None, [])
        acc = _swiglu_part(cur[0], nxt[1], wg_ref, wu_ref, wd_ref, chunks)
        for b in range(b0, b0 + ps):
            gt = mod_ref[mod_base + 2, b:b + 1, :]
            y = x_ref[b] + (0.5 * gt) * acc[(b - b0) * tq:(b - b0 + 1) * tq, :]
            for s in range(d // V7X_LANES):
                o_ref[0, s, pl.ds(b, tq, stride=nb), :] = y[:, s * V7X_LANES:(s + 1) * V7X_LANES]
        cur = nxt


def _ffn_out_body(*refs, mod_base, chunks, final):
    if final:
        x_ref, mod_ref, g_ref, wg_ref, wu_ref, wd_ref, fg_ref, o_ref, y_s = refs
    else:
        x_ref, mod_ref, g_ref, wg_ref, wu_ref, wd_ref, o_ref, y_s = refs
    nb, tq, d = o_ref.shape
    rows = nb * tq
    n_slab = d // V7X_LANES
    pr = min(rows, max(nb, FFN_PART_ROWS))
    assert rows % pr == 0 and pr % nb == 0
    tp = pr // nb
    sh = mod_ref[mod_base][None]
    sc = mod_ref[mod_base + 1][None]
    gt = mod_ref[mod_base + 2][None]

    def norm(r0):
        x = x_ref[0, r0:r0 + pr, :].reshape(tp, nb, d)
        h = (_rms(x, g_ref[...]) * (1.0 + sc) + sh).reshape(pr, d)
        toks = [h[j * GROUP:(j + 1) * GROUP, :V7X_LANES] for j in range(pr // GROUP)]
        return x, h.astype(BF16), toks

    cur = norm(0)
    for r0 in range(0, rows, pr):
        nxt = norm(r0 + pr) if r0 + pr < rows else (None, None, [])
        acc = _swiglu_part(cur[1], nxt[2], wg_ref, wu_ref, wd_ref, chunks)
        y = cur[0] + (0.5 * gt) * acc.reshape(tp, nb, d)
        if final:
            y = _rms(y, fg_ref[...])
        y = y.reshape(pr, d)
        for s in range(n_slab):
            y_s[s, r0:r0 + pr, :] = y[:, s * V7X_LANES:(s + 1) * V7X_LANES]
        t0 = r0 // nb
        for b in range(nb):
            o_ref[b, t0:t0 + tp, :] = jnp.concatenate(
                [y_s[s, pl.ds(r0 + b, tp, stride=nb), :] for s in range(n_slab)], axis=-1)
        cur = nxt


def _ffn_in(x, mod, g, wg, wu, wd, *, mod_base, tq):
    bsz, t, d = x.shape
    d_ff = wg.shape[1]
    n_grp, n_slab = bsz // GROUP, d // V7X_LANES
    body = functools.partial(
        _ffn_in_body, mod_base=mod_base, chunks=_ffn_chunks(d_ff, 4 * V7X_MXU_DIM))
    return pl.pallas_call(
        body,
        grid=(n_grp, t // tq),
        in_specs=[
            pl.BlockSpec((GROUP, tq, d), lambda b, i: (b, i, 0)),
            pl.BlockSpec((N_MOD, GROUP, d), lambda b, i: (0, b, 0)),
            _const_spec((1, d)),
            _const_spec((d, d_ff)),
            _const_spec((d, d_ff)),
            _const_spec((d_ff, d)),
        ],
        out_specs=pl.BlockSpec((1, n_slab, tq * GROUP, V7X_LANES), lambda b, i: (b, 0, i, 0)),
        out_shape=jax.ShapeDtypeStruct((n_grp, n_slab, t * GROUP, V7X_LANES), F32),
        compiler_params=_params(2),
        name="ffn_in",
    )(x, mod, g.reshape(1, d), wg, wu, wd)


def _ffn_out(x, mod, g, wg, wu, wd, final_g, *, mod_base, tq):
    n_grp, rows_all, d = x.shape
    t = rows_all // GROUP
    d_ff = wg.shape[1]
    final = final_g is not None
    body = functools.partial(
        _ffn_out_body, mod_base=mod_base, chunks=_ffn_chunks(d_ff, 4 * V7X_MXU_DIM), final=final)
    in_specs = [
        pl.BlockSpec((1, tq * GROUP, d), lambda b, i: (b, i, 0)),
        pl.BlockSpec((N_MOD, GROUP, d), lambda b, i: (0, b, 0)),
        _const_spec((1, d)),
        _const_spec((d, d_ff)),
        _const_spec((d, d_ff)),
        _const_spec((d_ff, d)),
    ]
    args = [x, mod, g.reshape(1, d), wg, wu, wd]
    if final:
        in_specs.append(_const_spec((1, d)))
        args.append(final_g.reshape(1, d))
    return pl.pallas_call(
        body,
        grid=(n_grp, t // tq),
        in_specs=in_specs,
        out_specs=pl.BlockSpec((GROUP, tq, d), lambda b, i: (b, i, 0)),
        out_shape=jax.ShapeDtypeStruct((n_grp * GROUP, t, d), F32),
        scratch_shapes=[pltpu.VMEM((d // V7X_LANES, tq * GROUP, V7X_LANES), F32)],
        compiler_params=_params(2),
        name="ffn_out_final" if final else "ffn_out",
    )(*args)


def _dwconv(buf_ref, w_ref, b_ref, row0, rows, n_taps):
    width = buf_ref.shape[-1] - V7X_LANES
    n_out = rows // GROUP
    first = row0 // GROUP
    block = min(CONV_BLOCK, n_out)
    assert n_out % block == 0
    col_blocks = []
    prev = None
    for c0 in range(0, width, V7X_LANES):
        lanes = slice(c0, c0 + V7X_LANES)
        outs = []
        for j0 in range(0, n_out, block):
            bias = jnp.broadcast_to(b_ref[:, lanes], (GROUP, V7X_LANES))
            if prev is not None:
                bias = bias + _zero_after(prev)
            accs = [bias] * block
            tiles = {}
            for k in range(n_taps):
                tap = jnp.broadcast_to(w_ref[k:k + 1, lanes], (GROUP, V7X_LANES))
                for r in range(block):
                    j = first + j0 + r + k
                    if j not in tiles:
                        tiles[j] = buf_ref[j * GROUP:(j + 1) * GROUP, lanes]
                    accs[r] = accs[r] + tap * tiles[j]
            prev = accs
            outs.extend(accs)
        col_blocks.append(jnp.concatenate(outs, axis=0))
    return jnp.concatenate(col_blocks, axis=-1)


def _mix_body(x_ref, mod_ref, h0_ref, lbuf0_ref, cbuf0_ref,
              mixg_ref, win_ref, lcw_ref, lcb_ref, wgate_ref, bgate_ref, lam_ref,
              cmw_ref, cmb_ref, lng_ref, lnb_ref, ogl_ref, ogc_ref, wout_ref,
              o_ref, ht_ref, ltail_ref, ctail_ref,
              lx_s, cu_s, h_s, hseq_s):
    t_idx = pl.program_id(1)
    rows, d = o_ref.shape[1:]
    tt = rows // GROUP
    dl = lx_s.shape[-1] - V7X_LANES
    dc = cu_s.shape[-1] - V7X_LANES
    hl_rows = lbuf0_ref.shape[1]
    hc_rows = cbuf0_ref.shape[1]
    n_gate_grp, gsz = wgate_ref.shape[0], wgate_ref.shape[1]

    @pl.when(t_idx == 0)
    def _load_state():
        lx_s[:hl_rows, :dl] = lbuf0_ref[0]
        cu_s[:hc_rows, :dc] = cbuf0_ref[0]
        h_s[...] = h0_ref[0]

    x = jnp.concatenate([x_ref[0, s] for s in range(x_ref.shape[1])], axis=-1)
    x3 = x.reshape(tt, GROUP, d)
    sh = mod_ref[3][None]
    sc = mod_ref[4][None]
    gt = mod_ref[5][None]
    hb = (_rms(x3, mixg_ref[...]) * (1.0 + sc) + sh).reshape(rows, d).astype(BF16)

    pr = MIX_PART_ROWS if rows % MIX_PART_ROWS == 0 else rows
    zs = [_dot(hb[r0:r0 + pr, :], win_ref[...]) for r0 in range(0, rows, pr)]
    carry = h_s[...]
    for p, r0 in enumerate(range(0, rows, pr)):
        z = zs[p]
        z_cv = z[:, :dc]
        z_cg = z[:, dc:2 * dc]
        z_lx = z[:, 2 * dc:2 * dc + dl]
        z_lg = z[:, 2 * dc + dl:]

        lx_s[hl_rows + r0:hl_rows + r0 + pr, :dl] = z_lx
        y = _dwconv(lx_s, lcw_ref, lcb_ref, r0, pr, hl_rows // GROUP + 1)
        yb = y.astype(BF16)
        r_parts, i_parts = [], []
        for gi in range(n_gate_grp):
            gates = _dot(yb[:, gi * gsz:(gi + 1) * gsz], wgate_ref[gi]) + bgate_ref[gi]
            r_parts.append(jax.nn.sigmoid(gates[:, :gsz]))
            i_parts.append(jax.nn.sigmoid(gates[:, gsz:]))
        r = jnp.concatenate(r_parts, axis=-1)
        i = jnp.concatenate(i_parts, axis=-1)
        log_a = -LRU_C * r * jax.nn.softplus(-lam_ref[...])
        a = jnp.exp(log_a)
        b = jnp.sqrt(-jnp.tanh(log_a) * (a * a + 1.0)) * (i * y)

        for j in range(pr // GROUP):
            rs = slice(j * GROUP, (j + 1) * GROUP)
            carry = a[rs, :] * carry + b[rs, :]
            hseq_s[r0 + j * GROUP:r0 + (j + 1) * GROUP, :] = carry
        lru_out = hseq_s[r0:r0 + pr, :] * jax.nn.gelu(z_lg)
        n_lru = _rms(lru_out, ogl_ref[...])

        cu_s[hc_rows + r0:hc_rows + r0 + pr, :dc] = z_cv * jax.nn.sigmoid(z_cg)
        v = _dwconv(cu_s, cmw_ref, cmb_ref, r0, pr, hc_rows // GROUP + 1)
        mu = jnp.mean(v, axis=-1, keepdims=True)
        vc = v - mu
        vn = vc * jax.lax.rsqrt(jnp.mean(vc * vc, axis=-1, keepdims=True) + EPS)
        vn = vn * lng_ref[...] + lnb_ref[...]
        vs = vn * jax.nn.sigmoid(vn)
        n_cv = _rms(vs, ogc_ref[...])

        merged = _dot(jnp.concatenate([n_cv, n_lru], axis=-1).astype(BF16), wout_ref[...])
        t0, t1 = r0 // GROUP, (r0 + pr) // GROUP
        o_ref[0, r0:r0 + pr, :] = (
            x3[t0:t1] + gt * merged.reshape(t1 - t0, GROUP, d)).reshape(pr, d)
    h_s[...] = carry

    lx_s[:hl_rows, :dl] = lx_s[rows:rows + hl_rows, :dl]
    cu_s[:hc_rows, :dc] = cu_s[rows:rows + hc_rows, :dc]
    ht_ref[0] = h_s[...]
    ltail_ref[0] = lx_s[:hl_rows, :dl]
    ctail_ref[0] = cu_s[:hc_rows, :dc]


def _mixer(x, mod, h0, lbuf, cbuf, w, *, tt):
    n_grp, n_slab, rows_all, _ = x.shape
    d = n_slab * V7X_LANES
    t = rows_all // GROUP
    rows = tt * GROUP
    dl, dc = h0.shape[-1], cbuf.shape[-1]
    hl_rows, hc_rows = lbuf.shape[1], cbuf.shape[1]
    assert t % tt == 0 and rows >= hc_rows and rows >= hl_rows

    def grp_spec(r, c):
        return pl.BlockSpec((1, r, c), lambda b, i: (b, 0, 0))

    consts = [w["mix_g"], w["w_in"], w["lru_conv_w"], w["lru_conv_b"], w["w_gate"], w["b_gate"],
              w["lru_lambda"], w["cm_dw_w"], w["cm_dw_b"], w["cm_ln_g"], w["cm_ln_b"],
              w["out_g_lru"], w["out_g_conv"], w["w_out"]]
    return pl.pallas_call(
        _mix_body,
        grid=(n_grp, t // tt),
        in_specs=[
            pl.BlockSpec((1, n_slab, rows, V7X_LANES), lambda b, i: (b, 0, i, 0)),
            pl.BlockSpec((N_MOD, GROUP, d), lambda b, i: (0, b, 0)),
            grp_spec(GROUP, dl), grp_spec(hl_rows, dl), grp_spec(hc_rows, dc),
        ] + [_const_spec(c.shape) for c in consts],
        out_specs=[
            pl.BlockSpec((1, rows, d), lambda b, i: (b, i, 0)),
            grp_spec(GROUP, dl), grp_spec(hl_rows, dl), grp_spec(hc_rows, dc),
        ],
        out_shape=[
            jax.ShapeDtypeStruct((n_grp, rows_all, d), F32),
            jax.ShapeDtypeStruct((n_grp, GROUP, dl), F32),
            jax.ShapeDtypeStruct((n_grp, hl_rows, dl), F32),
            jax.ShapeDtypeStruct((n_grp, hc_rows, dc), F32),
        ],
        scratch_shapes=[
            pltpu.VMEM((hl_rows + rows, dl + V7X_LANES), F32),
            pltpu.VMEM((hc_rows + rows, dc + V7X_LANES), F32),
            pltpu.VMEM((GROUP, dl), F32),
            pltpu.VMEM((rows, dl), F32),
        ],
        compiler_params=_params(2),
        name="mixer",
    )(x, mod, h0, lbuf, cbuf, *consts)


def _block_diag(w):
    heads, di, dj = w.shape
    eye = jnp.eye(heads, dtype=w.dtype)
    return jnp.einsum("hij,hk->hikj", w, eye).reshape(heads * di, heads * dj)


def _gate_weights(wa, wx, ba, bx):
    heads, hd, _ = wa.shape
    per = max(1, min(heads, V7X_MXU_DIM // hd))
    while heads % per:
        per -= 1
    n_grp, gsz = heads // per, per * hd
    ws, bs = [], []
    for gi in range(n_grp):
        hsl = slice(gi * per, (gi + 1) * per)
        csl = slice(gi * gsz, (gi + 1) * gsz)
        ws.append(jnp.concatenate([_block_diag(wa[hsl]), _block_diag(wx[hsl])], axis=1))
        bs.append(jnp.concatenate([ba[csl], bx[csl]]).reshape(1, -1))
    return jnp.stack(ws).astype(BF16), jnp.stack(bs)


def _to_groups(state):
    bsz, r, c = state.shape
    return state.reshape(bsz // GROUP, GROUP, r, c).transpose(0, 2, 1, 3).reshape(
        bsz // GROUP, r * GROUP, c)


def _from_groups(state, r):
    n_grp, _, c = state.shape
    return state.reshape(n_grp, r, GROUP, c).transpose(0, 2, 1, 3).reshape(n_grp * GROUP, r, c)


def _stream_layer(x, mod, h0, lbuf, cbuf, w, final_g):
    bsz, t, _ = x.shape
    assert bsz % GROUP == 0
    tq = min(t, 128)
    tt = min(t, 128)
    hl, hc = lbuf.shape[1], cbuf.shape[1]
    x = _ffn_in(x, mod, w["ffn1_g"], w["ffn1_wg"], w["ffn1_wu"], w["ffn1_wd"], mod_base=0, tq=tq)
    x, h_last, ltail, ctail = _mixer(
        x, mod, h0.reshape(bsz // GROUP, GROUP, -1), _to_groups(lbuf), _to_groups(cbuf), w, tt=tt)
    x = _ffn_out(x, mod, w["ffn2_g"], w["ffn2_wg"], w["ffn2_wu"], w["ffn2_wd"], final_g,
                 mod_base=6, tq=tq)
    return x, h_last.reshape(bsz, -1), _from_groups(ltail, hl), _from_groups(ctail, hc)


def kernel(x_prompt, x_sample, state_lru_h, state_lru_conv, state_cm_conv, c_prompt, c_sample, w_ada, b_ada, ffn1_g, ffn1_wg, ffn1_wu, ffn1_wd, mix_g, w_in, lru_conv_w, lru_conv_b, lru_wa, lru_ba, lru_wx, lru_bx, lru_lambda, cm_dw_w, cm_dw_b, cm_ln_g, cm_ln_b, out_g_lru, out_g_conv, w_out, ffn2_g, ffn2_wg, ffn2_wu, ffn2_wd, final_g):
    depth = w_ada.shape[0]
    if depth == 0:
        raise ValueError("at least one layer is required")
    bp = x_prompt.shape[0]
    dl = lru_lambda.shape[-1]
    dc = cm_dw_b.shape[-1]
    hl = lru_conv_w.shape[1] - 1
    hc = cm_dw_w.shape[1] - 1
    xp, xs = x_prompt, x_sample
    outs = [[] for _ in range(6)]
    for l in range(depth):
        row = lambda v: v[l].reshape(1, -1)
        w_gate, b_gate = _gate_weights(lru_wa[l], lru_wx[l], lru_ba[l], lru_bx[l])
        w = {
            "ffn1_g": ffn1_g[l], "ffn1_wg": ffn1_wg[l].astype(BF16),
            "ffn1_wu": ffn1_wu[l].astype(BF16), "ffn1_wd": ffn1_wd[l].astype(BF16),
            "ffn2_g": ffn2_g[l], "ffn2_wg": ffn2_wg[l].astype(BF16),
            "ffn2_wu": ffn2_wu[l].astype(BF16), "ffn2_wd": ffn2_wd[l].astype(BF16),
            "mix_g": row(mix_g),
            "w_in": jnp.concatenate([w_in[l][:, 2 * dl:], w_in[l][:, :2 * dl]], axis=1).astype(BF16),
            "w_out": jnp.concatenate([w_out[l][dl:], w_out[l][:dl]], axis=0).astype(BF16),
            "lru_conv_w": lru_conv_w[l], "lru_conv_b": row(lru_conv_b),
            "w_gate": w_gate, "b_gate": b_gate,
            "lru_lambda": row(lru_lambda),
            "cm_dw_w": cm_dw_w[l], "cm_dw_b": row(cm_dw_b),
            "cm_ln_g": row(cm_ln_g), "cm_ln_b": row(cm_ln_b),
            "out_g_lru": row(out_g_lru), "out_g_conv": row(out_g_conv),
        }
        fg = final_g if l == depth - 1 else None
        mod = _ada(jnp.concatenate([c_prompt, c_sample], axis=0), w_ada[l], b_ada[l])
        xp, hp, lcp, ccp = _stream_layer(
            xp, mod[:, :bp], jnp.zeros((bp, dl), F32), jnp.zeros((bp, hl, dl), F32),
            jnp.zeros((bp, hc, dc), F32), w, fg)
        xs, hs, lcs, ccs = _stream_layer(
            xs, mod[:, bp:], state_lru_h[l], state_lru_conv[l], state_cm_conv[l], w, fg)
        for lst, v in zip(outs, (hp, lcp, ccp, hs, lcs, ccs)):
            lst.append(v)
    return (xp, xs) + tuple(jnp.stack(v, axis=0) for v in outs)
```

```python
import functools

import jax
import jax.numpy as jnp
from jax.experimental import pallas as pl
from jax.experimental.pallas import tpu as pltpu

EPS = 1e-6
LRU_C = 8.0
N_MOD = 9

V7X_SUBLANES = 8
V7X_LANES = 128
V7X_MXU_DIM = 256
V7X_VMEM_LIMIT_BYTES = 56 * 1024 * 1024

GROUP = V7X_SUBLANES
CONV_BLOCK = 8
MIX_PART_ROWS = 256

F32 = jnp.float32
BF16 = jnp.bfloat16


def _dot(a, b):
    return jnp.dot(a, b, preferred_element_type=F32)


def _rms(xf, g):
    return xf * jax.lax.rsqrt(jnp.mean(xf * xf, axis=-1, keepdims=True) + EPS) * g


def _const_spec(shape):
    zeros = (0,) * len(shape)
    return pl.BlockSpec(shape, lambda *_: zeros, pipeline_mode=pl.Buffered(1))


def _params(n_grid):
    return pltpu.CompilerParams(
        dimension_semantics=("arbitrary",) * n_grid, vmem_limit_bytes=V7X_VMEM_LIMIT_BYTES)


def _ada_body(c_ref, w_ref, b_ref, o_ref):
    c = c_ref[...]
    s = (c * jax.nn.sigmoid(c)).astype(BF16)
    o_ref[0] = _dot(s, w_ref[...].astype(BF16)) + b_ref[...]


def _ada(c, w_ada, b_ada):
    bsz, d = c.shape
    n_mod = w_ada.shape[1] // d
    return pl.pallas_call(
        _ada_body,
        grid=(n_mod,),
        in_specs=[
            pl.BlockSpec((bsz, d), lambda j: (0, 0)),
            pl.BlockSpec((d, d), lambda j: (0, j)),
            pl.BlockSpec((1, d), lambda j: (0, j)),
        ],
        out_specs=pl.BlockSpec((1, bsz, d), lambda j: (j, 0, 0)),
        out_shape=jax.ShapeDtypeStruct((n_mod, bsz, d), F32),
        compiler_params=_params(1),
        name="ada_mod",
    )(c, w_ada, b_ada.reshape(1, -1))


def _ffn_chunks(d_ff, width):
    out, lo = [], 0
    while lo < d_ff:
        sz = min(width, d_ff - lo)
        out.append((lo, sz))
        lo += sz
    return tuple(out)


FFN_PART_ROWS = 256


def _zero_after(tiles):
    bits = None
    for t in tiles:
        w = jax.lax.shift_right_logical(pltpu.bitcast(t, jnp.uint32), jnp.uint32(16))
        w = jax.lax.shift_right_logical(w, jnp.uint32(16))
        bits = w if bits is None else bits | w
    return pltpu.bitcast(bits, F32)


def _after(x, tokens):
    zero = _zero_after(tokens)
    reps = x.shape[0] // GROUP
    head = x[:, :V7X_LANES] + jnp.concatenate([zero] * reps, axis=0)
    return jnp.concatenate([head, x[:, V7X_LANES:]], axis=-1)


def _swiglu_part(hb, next_tokens, wg_ref, wu_ref, wd_ref, chunks, act_ref):
    for ci, (lo, sz) in enumerate(chunks):
        g = _dot(hb, wg_ref[:, lo:lo + sz])
        u = _dot(hb, wu_ref[:, lo:lo + sz])
        act = g * jax.nn.sigmoid(g) * u
        if ci == 0 and next_tokens:
            act = _after(act, next_tokens)
        act_ref[:, lo:lo + sz] = act.astype(BF16)
    return _dot(act_ref[...], wd_ref[...])


def _ffn_in_body(x_ref, mod_ref, g_ref, wg_ref, wu_ref, wd_ref, o_ref, act_s, *, mod_base, chunks):
    nb, tq, d = x_ref.shape
    ps = min(nb, max(1, FFN_PART_ROWS // tq))
    assert nb % ps == 0

    def norm(b0):
        hs, toks = [], []
        for b in range(b0, b0 + ps):
            sh = mod_ref[mod_base, b:b + 1, :]
            sc = mod_ref[mod_base + 1, b:b + 1, :]
            h = _rms(x_ref[b], g_ref[...]) * (1.0 + sc) + sh
            toks.extend(h[j * GROUP:(j + 1) * GROUP, :V7X_LANES] for j in range(tq // GROUP))
            hs.append(h.astype(BF16))
        return jnp.concatenate(hs, axis=0), toks

    cur = norm(0)
    for b0 in range(0, nb, ps):
        nxt = norm(b0 + ps) if b0 + ps < nb else (None, [])
        acc = _swiglu_part(cur[0], nxt[1], wg_ref, wu_ref, wd_ref, chunks,
                           act_s.at[(b0 // ps) % 2])
        for b in range(b0, b0 + ps):
            gt = mod_ref[mod_base + 2, b:b + 1, :]
            y = x_ref[b] + (0.5 * gt) * acc[(b - b0) * tq:(b - b0 + 1) * tq, :]
            for s in range(d // V7X_LANES):
                o_ref[0, s, pl.ds(b, tq, stride=nb), :] = y[:, s * V7X_LANES:(s + 1) * V7X_LANES]
        cur = nxt


def _ffn_out_body(*refs, mod_base, chunks, final):
    if final:
        x_ref, mod_ref, g_ref, wg_ref, wu_ref, wd_ref, fg_ref, o_ref, y_s, act_s = refs
    else:
        x_ref, mod_ref, g_ref, wg_ref, wu_ref, wd_ref, o_ref, y_s, act_s = refs
    nb, tq, d = o_ref.shape
    rows = nb * tq
    n_slab = d // V7X_LANES
    pr = min(rows, max(nb, FFN_PART_ROWS))
    assert rows % pr == 0 and pr % nb == 0
    tp = pr // nb
    sh = mod_ref[mod_base][None]
    sc = mod_ref[mod_base + 1][None]
    gt = mod_ref[mod_base + 2][None]

    def norm(r0):
        x = x_ref[0, r0:r0 + pr, :].reshape(tp, nb, d)
        h = (_rms(x, g_ref[...]) * (1.0 + sc) + sh).reshape(pr, d)
        toks = [h[j * GROUP:(j + 1) * GROUP, :V7X_LANES] for j in range(pr // GROUP)]
        return x, h.astype(BF16), toks

    cur = norm(0)
    for r0 in range(0, rows, pr):
        nxt = norm(r0 + pr) if r0 + pr < rows else (None, None, [])
        acc = _swiglu_part(cur[1], nxt[2], wg_ref, wu_ref, wd_ref, chunks,
                           act_s.at[(r0 // pr) % 2])
        y = cur[0] + (0.5 * gt) * acc.reshape(tp, nb, d)
        if final:
            y = _rms(y, fg_ref[...])
        y = y.reshape(pr, d)
        for s in range(n_slab):
            y_s[s, r0:r0 + pr, :] = y[:, s * V7X_LANES:(s + 1) * V7X_LANES]
        t0 = r0 // nb
        for b in range(nb):
            o_ref[b, t0:t0 + tp, :] = jnp.concatenate(
                [y_s[s, pl.ds(r0 + b, tp, stride=nb), :] for s in range(n_slab)], axis=-1)
        cur = nxt


def _ffn_in(x, mod, g, wg, wu, wd, *, mod_base, tq):
    bsz, t, d = x.shape
    d_ff = wg.shape[1]
    n_grp, n_slab = bsz // GROUP, d // V7X_LANES
    body = functools.partial(
        _ffn_in_body, mod_base=mod_base, chunks=_ffn_chunks(d_ff, 4 * V7X_MXU_DIM))
    return pl.pallas_call(
        body,
        grid=(n_grp, t // tq),
        in_specs=[
            pl.BlockSpec((GROUP, tq, d), lambda b, i: (b, i, 0)),
            pl.BlockSpec((N_MOD, GROUP, d), lambda b, i: (0, b, 0)),
            _const_spec((1, d)),
            _const_spec((d, d_ff)),
            _const_spec((d, d_ff)),
            _const_spec((d_ff, d)),
        ],
        out_specs=pl.BlockSpec((1, n_slab, tq * GROUP, V7X_LANES), lambda b, i: (b, 0, i, 0)),
        out_shape=jax.ShapeDtypeStruct((n_grp, n_slab, t * GROUP, V7X_LANES), F32),
        scratch_shapes=[pltpu.VMEM((2, min(GROUP * tq, FFN_PART_ROWS), d_ff), BF16)],
        compiler_params=_params(2),
        name="ffn_in",
    )(x, mod, g.reshape(1, d), wg, wu, wd)


def _ffn_out(x, mod, g, wg, wu, wd, final_g, *, mod_base, tq):
    n_grp, rows_all, d = x.shape
    t = rows_all // GROUP
    d_ff = wg.shape[1]
    final = final_g is not None
    body = functools.partial(
        _ffn_out_body, mod_base=mod_base, chunks=_ffn_chunks(d_ff, 4 * V7X_MXU_DIM), final=final)
    in_specs = [
        pl.BlockSpec((1, tq * GROUP, d), lambda b, i: (b, i, 0)),
        pl.BlockSpec((N_MOD, GROUP, d), lambda b, i: (0, b, 0)),
        _const_spec((1, d)),
        _const_spec((d, d_ff)),
        _const_spec((d, d_ff)),
        _const_spec((d_ff, d)),
    ]
    args = [x, mod, g.reshape(1, d), wg, wu, wd]
    if final:
        in_specs.append(_const_spec((1, d)))
        args.append(final_g.reshape(1, d))
    return pl.pallas_call(
        body,
        grid=(n_grp, t // tq),
        in_specs=in_specs,
        out_specs=pl.BlockSpec((GROUP, tq, d), lambda b, i: (b, i, 0)),
        out_shape=jax.ShapeDtypeStruct((n_grp * GROUP, t, d), F32),
        scratch_shapes=[pltpu.VMEM((d // V7X_LANES, tq * GROUP, V7X_LANES), F32),
                        pltpu.VMEM((2, min(GROUP * tq, FFN_PART_ROWS), d_ff), BF16)],
        compiler_params=_params(2),
        name="ffn_out_final" if final else "ffn_out",
    )(*args)


def _dwconv(buf_ref, w_ref, b_ref, row0, rows, n_taps):
    width = buf_ref.shape[-1]
    n_out = rows // GROUP
    first = row0 // GROUP
    block = min(CONV_BLOCK, n_out)
    assert n_out % block == 0
    col_blocks = []
    prev = None
    for c0 in range(0, width, V7X_LANES):
        lanes = slice(c0, c0 + V7X_LANES)
        outs = []
        for j0 in range(0, n_out, block):
            bias = jnp.broadcast_to(b_ref[:, lanes], (GROUP, V7X_LANES))
            if prev is not None:
                bias = bias + _zero_after(prev)
            accs = [bias] * block
            tiles = {}
            for k in range(n_taps):
                tap = jnp.broadcast_to(w_ref[k:k + 1, lanes], (GROUP, V7X_LANES))
                for r in range(block):
                    j = first + j0 + r + k
                    if j not in tiles:
                        tiles[j] = buf_ref[j * GROUP:(j + 1) * GROUP, lanes]
                    accs[r] = accs[r] + tap * tiles[j]
            prev = accs
            outs.extend(accs)
        col_blocks.append(jnp.concatenate(outs, axis=0))
    return jnp.concatenate(col_blocks, axis=-1)


def _mix_body(x_ref, mod_ref, h0_ref, lbuf0_ref, cbuf0_ref,
              mixg_ref, win_ref, lcw_ref, lcb_ref, wgate_ref, bgate_ref, lam_ref,
              cmw_ref, cmb_ref, lng_ref, lnb_ref, ogl_ref, ogc_ref, wout_ref,
              o_ref, ht_ref, ltail_ref, ctail_ref,
              lx_s, cu_s, h_s, hseq_s):
    t_idx = pl.program_id(1)
    rows, d = o_ref.shape[1:]
    tt = rows // GROUP
    dl = lx_s.shape[-1]
    dc = cu_s.shape[-1]
    hl_rows = lbuf0_ref.shape[1]
    hc_rows = cbuf0_ref.shape[1]
    n_gate_grp, gsz = wgate_ref.shape[0], wgate_ref.shape[1]

    @pl.when(t_idx == 0)
    def _load_state():
        lx_s[:hl_rows, :] = lbuf0_ref[0]
        cu_s[:hc_rows, :] = cbuf0_ref[0]
        h_s[...] = h0_ref[0]

    x = jnp.concatenate([x_ref[0, s] for s in range(x_ref.shape[1])], axis=-1)
    x3 = x.reshape(tt, GROUP, d)
    sh = mod_ref[3][None]
    sc = mod_ref[4][None]
    gt = mod_ref[5][None]
    hb = (_rms(x3, mixg_ref[...]) * (1.0 + sc) + sh).reshape(rows, d).astype(BF16)

    pr = MIX_PART_ROWS if rows % MIX_PART_ROWS == 0 else rows
    zs = [_dot(hb[r0:r0 + pr, :], win_ref[...]) for r0 in range(0, rows, pr)]
    carry = h_s[...]
    for p, r0 in enumerate(range(0, rows, pr)):
        z = zs[p]
        z_cv = z[:, :dc]
        z_cg = z[:, dc:2 * dc]
        z_lx = z[:, 2 * dc:2 * dc + dl]
        z_lg = z[:, 2 * dc + dl:]

        lx_s[hl_rows + r0:hl_rows + r0 + pr, :] = z_lx
        y = _dwconv(lx_s, lcw_ref, lcb_ref, r0, pr, hl_rows // GROUP + 1)
        yb = y.astype(BF16)
        r_parts, i_parts = [], []
        for gi in range(n_gate_grp):
            gates = _dot(yb[:, gi * gsz:(gi + 1) * gsz], wgate_ref[gi]) + bgate_ref[gi]
            r_parts.append(jax.nn.sigmoid(gates[:, :gsz]))
            i_parts.append(jax.nn.sigmoid(gates[:, gsz:]))
        r = jnp.concatenate(r_parts, axis=-1)
        i = jnp.concatenate(i_parts, axis=-1)
        log_a = -LRU_C * r * jax.nn.softplus(-lam_ref[...])
        a = jnp.exp(log_a)
        b = jnp.sqrt(-jnp.tanh(log_a) * (a * a + 1.0)) * (i * y)

        for j in range(pr // GROUP):
            rs = slice(j * GROUP, (j + 1) * GROUP)
            carry = a[rs, :] * carry + b[rs, :]
            hseq_s[r0 + j * GROUP:r0 + (j + 1) * GROUP, :] = carry
        lru_out = hseq_s[r0:r0 + pr, :] * jax.nn.gelu(z_lg)
        n_lru = _rms(lru_out, ogl_ref[...])

        cu_s[hc_rows + r0:hc_rows + r0 + pr, :] = z_cv * jax.nn.sigmoid(z_cg)
        v = _dwconv(cu_s, cmw_ref, cmb_ref, r0, pr, hc_rows // GROUP + 1)
        mu = jnp.mean(v, axis=-1, keepdims=True)
        vc = v - mu
        vn = vc * jax.lax.rsqrt(jnp.mean(vc * vc, axis=-1, keepdims=True) + EPS)
        vn = vn * lng_ref[...] + lnb_ref[...]
        vs = vn * jax.nn.sigmoid(vn)
        n_cv = _rms(vs, ogc_ref[...])

        merged = _dot(jnp.concatenate([n_cv, n_lru], axis=-1).astype(BF16), wout_ref[...])
        t0, t1 = r0 // GROUP, (r0 + pr) // GROUP
        o_ref[0, r0:r0 + pr, :] = (
            x3[t0:t1] + gt * merged.reshape(t1 - t0, GROUP, d)).reshape(pr, d)
    h_s[...] = carry

    lx_s[:hl_rows, :] = lx_s[rows:rows + hl_rows, :]
    cu_s[:hc_rows, :] = cu_s[rows:rows + hc_rows, :]
    ht_ref[0] = h_s[...]
    ltail_ref[0] = lx_s[:hl_rows, :]
    ctail_ref[0] = cu_s[:hc_rows, :]


def _mixer(x, mod, h0, lbuf, cbuf, w, *, tt):
    n_grp, n_slab, rows_all, _ = x.shape
    d = n_slab * V7X_LANES
    t = rows_all // GROUP
    rows = tt * GROUP
    dl, dc = h0.shape[-1], cbuf.shape[-1]
    hl_rows, hc_rows = lbuf.shape[1], cbuf.shape[1]
    assert t % tt == 0 and rows >= hc_rows and rows >= hl_rows

    def grp_spec(r, c):
        return pl.BlockSpec((1, r, c), lambda b, i: (b, 0, 0))

    consts = [w["mix_g"], w["w_in"], w["lru_conv_w"], w["lru_conv_b"], w["w_gate"], w["b_gate"],
              w["lru_lambda"], w["cm_dw_w"], w["cm_dw_b"], w["cm_ln_g"], w["cm_ln_b"],
              w["out_g_lru"], w["out_g_conv"], w["w_out"]]
    return pl.pallas_call(
        _mix_body,
        grid=(n_grp, t // tt),
        in_specs=[
            pl.BlockSpec((1, n_slab, rows, V7X_LANES), lambda b, i: (b, 0, i, 0)),
            pl.BlockSpec((N_MOD, GROUP, d), lambda b, i: (0, b, 0)),
            grp_spec(GROUP, dl), grp_spec(hl_rows, dl), grp_spec(hc_rows, dc),
        ] + [_const_spec(c.shape) for c in consts],
        out_specs=[
            pl.BlockSpec((1, rows, d), lambda b, i: (b, i, 0)),
            grp_spec(GROUP, dl), grp_spec(hl_rows, dl), grp_spec(hc_rows, dc),
        ],
        out_shape=[
            jax.ShapeDtypeStruct((n_grp, rows_all, d), F32),
            jax.ShapeDtypeStruct((n_grp, GROUP, dl), F32),
            jax.ShapeDtypeStruct((n_grp, hl_rows, dl), F32),
            jax.ShapeDtypeStruct((n_grp, hc_rows, dc), F32),
        ],
        scratch_shapes=[
            pltpu.VMEM((hl_rows + rows, dl), F32),
            pltpu.VMEM((hc_rows + rows, dc), F32),
            pltpu.VMEM((GROUP, dl), F32),
            pltpu.VMEM((rows, dl), F32),
        ],
        compiler_params=_params(2),
        name="mixer",
    )(x, mod, h0, lbuf, cbuf, *consts)


def _block_diag(w):
    heads, di, dj = w.shape
    eye = jnp.eye(heads, dtype=w.dtype)
    return jnp.einsum("hij,hk->hikj", w, eye).reshape(heads * di, heads * dj)


def _gate_weights(wa, wx, ba, bx):
    heads, hd, _ = wa.shape
    per = max(1, min(heads, V7X_MXU_DIM // hd))
    while heads % per:
        per -= 1
    n_grp, gsz = heads // per, per * hd
    ws, bs = [], []
    for gi in range(n_grp):
        hsl = slice(gi * per, (gi + 1) * per)
        csl = slice(gi * gsz, (gi + 1) * gsz)
        ws.append(jnp.concatenate([_block_diag(wa[hsl]), _block_diag(wx[hsl])], axis=1))
        bs.append(jnp.concatenate([ba[csl], bx[csl]]).reshape(1, -1))
    return jnp.stack(ws).astype(BF16), jnp.stack(bs)


def _to_groups(state):
    bsz, r, c = state.shape
    return state.reshape(bsz // GROUP, GROUP, r, c).transpose(0, 2, 1, 3).reshape(
        bsz // GROUP, r * GROUP, c)


def _from_groups(state, r):
    n_grp, _, c = state.shape
    return state.reshape(n_grp, r, GROUP, c).transpose(0, 2, 1, 3).reshape(n_grp * GROUP, r, c)


def _stream_layer(x, mod, h0, lbuf, cbuf, w, final_g):
    bsz, t, _ = x.shape
    assert bsz % GROUP == 0
    tq = min(t, 128)
    tt = min(t, 128)
    hl, hc = lbuf.shape[1], cbuf.shape[1]
    x = _ffn_in(x, mod, w["ffn1_g"], w["ffn1_wg"], w["ffn1_wu"], w["ffn1_wd"], mod_base=0, tq=tq)
    x, h_last, ltail, ctail = _mixer(
        x, mod, h0.reshape(bsz // GROUP, GROUP, -1), _to_groups(lbuf), _to_groups(cbuf), w, tt=tt)
    x = _ffn_out(x, mod, w["ffn2_g"], w["ffn2_wg"], w["ffn2_wu"], w["ffn2_wd"], final_g,
                 mod_base=6, tq=tq)
    return x, h_last.reshape(bsz, -1), _from_groups(ltail, hl), _from_groups(ctail, hc)


def kernel(x_prompt, x_sample, state_lru_h, state_lru_conv, state_cm_conv, c_prompt, c_sample, w_ada, b_ada, ffn1_g, ffn1_wg, ffn1_wu, ffn1_wd, mix_g, w_in, lru_conv_w, lru_conv_b, lru_wa, lru_ba, lru_wx, lru_bx, lru_lambda, cm_dw_w, cm_dw_b, cm_ln_g, cm_ln_b, out_g_lru, out_g_conv, w_out, ffn2_g, ffn2_wg, ffn2_wu, ffn2_wd, final_g):
    depth = w_ada.shape[0]
    if depth == 0:
        raise ValueError("at least one layer is required")
    bp = x_prompt.shape[0]
    dl = lru_lambda.shape[-1]
    dc = cm_dw_b.shape[-1]
    hl = lru_conv_w.shape[1] - 1
    hc = cm_dw_w.shape[1] - 1
    xp, xs = x_prompt, x_sample
    outs = [[] for _ in range(6)]
    for l in range(depth):
        row = lambda v: v[l].reshape(1, -1)
        w_gate, b_gate = _gate_weights(lru_wa[l], lru_wx[l], lru_ba[l], lru_bx[l])
        w = {
            "ffn1_g": ffn1_g[l], "ffn1_wg": ffn1_wg[l].astype(BF16),
            "ffn1_wu": ffn1_wu[l].astype(BF16), "ffn1_wd": ffn1_wd[l].astype(BF16),
            "ffn2_g": ffn2_g[l], "ffn2_wg": ffn2_wg[l].astype(BF16),
            "ffn2_wu": ffn2_wu[l].astype(BF16), "ffn2_wd": ffn2_wd[l].astype(BF16),
            "mix_g": row(mix_g),
            "w_in": jnp.concatenate([w_in[l][:, 2 * dl:], w_in[l][:, :2 * dl]], axis=1).astype(BF16),
            "w_out": jnp.concatenate([w_out[l][dl:], w_out[l][:dl]], axis=0).astype(BF16),
            "lru_conv_w": lru_conv_w[l], "lru_conv_b": row(lru_conv_b),
            "w_gate": w_gate, "b_gate": b_gate,
            "lru_lambda": row(lru_lambda),
            "cm_dw_w": cm_dw_w[l], "cm_dw_b": row(cm_dw_b),
            "cm_ln_g": row(cm_ln_g), "cm_ln_b": row(cm_ln_b),
            "out_g_lru": row(out_g_lru), "out_g_conv": row(out_g_conv),
        }
        fg = final_g if l == depth - 1 else None
        mod = _ada(jnp.concatenate([c_prompt, c_sample], axis=0), w_ada[l], b_ada[l])
        xp, hp, lcp, ccp = _stream_layer(
            xp, mod[:, :bp], jnp.zeros((bp, dl), F32), jnp.zeros((bp, hl, dl), F32),
            jnp.zeros((bp, hc, dc), F32), w, fg)
        xs, hs, lcs, ccs = _stream_layer(
            xs, mod[:, bp:], state_lru_h[l], state_lru_conv[l], state_cm_conv[l], w, fg)
        for lst, v in zip(outs, (hp, lcp, ccp, hs, lcs, ccs)):
            lst.append(v)
    return (xp, xs) + tuple(jnp.stack(v, axis=0) for v in outs)
```
